```python
import math
import jax
import jax.numpy as jnp
from jax import lax
import numpy as np

D_MODEL = 1024
BATCH = 2
SEQ = 8192
DEPTH = 1

HEAD_DIM = 64
SB_HEADS = 8
DIL_PATTERNS = ((128, 1), (512, 4), (2048, 16))
DIL_HEADS_PER_GROUP = 4
DIL_HEADS = DIL_HEADS_PER_GROUP * len(DIL_PATTERNS)
Q_BLOCK = 128
SB_WIDTH = SB_HEADS * HEAD_DIM
DIL_WIDTH = DIL_HEADS * HEAD_DIM
DIL_OUT_WIDTH = DIL_HEADS_PER_GROUP * HEAD_DIM
IN_SPLITS = (SB_WIDTH, SB_WIDTH, SB_WIDTH, DIL_WIDTH, DIL_WIDTH, DIL_WIDTH, D_MODEL, D_MODEL)
IN_WIDTH = int(sum(IN_SPLITS))
IN_SPLIT_POINTS = tuple(int(v) for v in np.cumsum(IN_SPLITS)[:-1])

REL_BUCKETS = 32
REL_MAX_DISTANCE = 2048

N_EXPERTS = 256
TOP_K = 8
N_GROUPS = 8
TOPK_GROUPS = 4
EXPERT_HIDDEN = 256
SHARED_HIDDEN = 256
ROUTED_SCALE = 2.5
EXPERT_BLOCK = 128

LN_EPS = 1e-5
DN_ALPHA = (2 * DEPTH) ** 0.25
DN_BETA = (8 * DEPTH) ** -0.25

kernel_name = "hybrid_stickbreak_dilated_moe_block"


def _layer_norm(x, g, b):
    xf = x.astype(jnp.float32)
    mu = jnp.mean(xf, axis=-1, keepdims=True)
    var = jnp.mean(jnp.square(xf - mu), axis=-1, keepdims=True)
    y = (xf - mu) * lax.rsqrt(var + LN_EPS)
    return (y * g.astype(jnp.float32) + b.astype(jnp.float32)).astype(x.dtype)


def _split_heads(t, n_heads):
    b, s, _ = t.shape
    return t.reshape(b, s, n_heads, HEAD_DIM).transpose(0, 2, 1, 3)


def _merge_heads(t):
    b, h, s, d = t.shape
    return t.transpose(0, 2, 1, 3).reshape(b, s, h * d)


def _to_query_blocks(t):
    b, h, s, d = t.shape
    return t.reshape(b, h, s // Q_BLOCK, Q_BLOCK, d).transpose(2, 0, 1, 3, 4)


def _from_query_blocks(t):
    nb, b, h, q, d = t.shape
    return t.transpose(1, 2, 0, 3, 4).reshape(b, h, nb * q, d)


def _stick_breaking_attention(q, k, v):
    s = q.shape[2]
    scale = 1.0 / math.sqrt(HEAD_DIM)
    key_pos = jnp.arange(s, dtype=jnp.int32)

    def one_block(args):
        q_blk, blk = args
        q_pos = blk * Q_BLOCK + jnp.arange(Q_BLOCK, dtype=jnp.int32)
        before = key_pos[None, :] < q_pos[:, None]
        z = jnp.einsum('bhqd,bhkd->bhqk', q_blk, k).astype(jnp.float32) * scale
        log_keep = jnp.where(before, jax.nn.log_sigmoid(-z), 0.0)
        log_keep_between = lax.cumsum(log_keep, axis=3, reverse=True) - log_keep
        weight = jnp.where(before, jnp.exp(jax.nn.log_sigmoid(z) + log_keep_between), 0.0)
        return jnp.einsum('bhqk,bhkd->bhqd', weight.astype(v.dtype), v)

    n_blocks = s // Q_BLOCK
    out = lax.map(one_block, (_to_query_blocks(q), jnp.arange(n_blocks, dtype=jnp.int32)))
    return _from_query_blocks(out)


def _rel_bucket(dist):
    max_exact = REL_BUCKETS // 2
    d = jnp.maximum(dist, 1).astype(jnp.float32)
    large = max_exact + (jnp.log(d / max_exact) / math.log(REL_MAX_DISTANCE / max_exact)
                         * (REL_BUCKETS - max_exact)).astype(jnp.int32)
    large = jnp.minimum(large, REL_BUCKETS - 1)
    return jnp.where(dist < max_exact, dist, large)


def _dilated_attention(q, k, v, rel_bias):
    s = q.shape[2]
    scale = 1.0 / math.sqrt(HEAD_DIM)
    n_blocks = s // Q_BLOCK
    groups = []
    for g, (window, dilation) in enumerate(DIL_PATTERNS):
        offsets = jnp.arange(window // dilation + 1, dtype=jnp.int32) * dilation
        heads = slice(g * DIL_HEADS_PER_GROUP, (g + 1) * DIL_HEADS_PER_GROUP)
        bias = rel_bias[_rel_bucket(offsets)][:, heads].T.astype(jnp.float32)
        groups.append((heads, offsets, bias, k[:, heads], v[:, heads]))

    def one_block(args):
        q_blk, blk = args
        q_pos = blk * Q_BLOCK + jnp.arange(Q_BLOCK, dtype=jnp.int32)
        outs, lses = [], []
        for heads, offsets, bias, k_g, v_g in groups:
            idx = q_pos[:, None] - offsets[None, :]
            valid = idx >= 0
            idx = jnp.maximum(idx, 0)
            k_sel = jnp.take(k_g, idx, axis=2)
            v_sel = jnp.take(v_g, idx, axis=2)
            logits = (jnp.einsum('bhqd,bhqmd->bhqm', q_blk[:, heads], k_sel).astype(jnp.float32) * scale
                      + bias[:, None, :])
            logits = jnp.where(valid, logits, -jnp.inf)
            lse = jax.nn.logsumexp(logits, axis=-1)
            probs = jnp.exp(logits - lse[..., None])
            outs.append(jnp.einsum('bhqm,bhqmd->bhqd', probs.astype(v_g.dtype), v_sel).astype(jnp.float32))
            lses.append(lse)
        mix = jax.nn.softmax(jnp.stack(lses), axis=0)
        out = jnp.sum(mix[..., None] * jnp.stack(outs), axis=0)
        return out.astype(q_blk.dtype)

    out = lax.map(one_block, (_to_query_blocks(q), jnp.arange(n_blocks, dtype=jnp.int32)))
    return _from_query_blocks(out)


def _mixer(h, w_in, b_gate, w_br_sb, w_br_dil, w_out, rel_bias):
    proj = jnp.einsum('bsd,df->bsf', h, w_in)
    q_sb, k_sb, v_sb, q_dl, k_dl, v_dl, g_sb, g_dl = jnp.split(proj, IN_SPLIT_POINTS, axis=-1)
    o_sb = _stick_breaking_attention(_split_heads(q_sb, SB_HEADS), _split_heads(k_sb, SB_HEADS),
                                     _split_heads(v_sb, SB_HEADS))
    o_dl = _dilated_attention(_split_heads(q_dl, DIL_HEADS), _split_heads(k_dl, DIL_HEADS),
                              _split_heads(v_dl, DIL_HEADS), rel_bias)
    gates = jax.nn.sigmoid(jnp.concatenate([g_sb, g_dl], axis=-1) + b_gate)
    gate_sb, gate_dl = jnp.split(gates, 2, axis=-1)
    merged = (gate_sb * jnp.einsum('bsf,fd->bsd', _merge_heads(o_sb), w_br_sb)
              + gate_dl * jnp.einsum('bsf,fd->bsd', _merge_heads(o_dl), w_br_dil))
    return jnp.einsum('bsd,de->bse', merged, w_out)


def _route(h_flat, w_router, router_bias):
    n_tok = h_flat.shape[0]
    scores = jax.nn.sigmoid(h_flat.astype(jnp.float32) @ w_router.astype(jnp.float32))
    biased = scores + router_bias.astype(jnp.float32)
    grp = biased.reshape(n_tok, N_GROUPS, N_EXPERTS // N_GROUPS)
    grp_score = jnp.sum(lax.top_k(grp, 2)[0], axis=-1)
    _, top_groups = lax.top_k(grp_score, TOPK_GROUPS)
    grp_mask = jnp.any(top_groups[..., None] == jnp.arange(N_GROUPS, dtype=top_groups.dtype), axis=1)
    expert_mask = jnp.repeat(grp_mask, N_EXPERTS // N_GROUPS, axis=1)
    _, idx = lax.top_k(jnp.where(expert_mask, biased, -jnp.inf), TOP_K)
    w = jnp.take_along_axis(scores, idx, axis=1)
    w = w / jnp.sum(w, axis=-1, keepdims=True) * ROUTED_SCALE
    return idx.astype(jnp.int32), w


def _routed_experts(h_flat, idx, w, w_gate_e, w_up_e, w_down_e):
    n_tok, d = h_flat.shape
    n_assign = n_tok * TOP_K
    e_flat = idx.reshape(-1)
    tok_flat = jnp.arange(n_assign, dtype=jnp.int32) // TOP_K
    g_flat = w.reshape(-1)
    order = jnp.argsort(e_flat)
    e_sorted = e_flat[order]
    counts = jnp.zeros((N_EXPERTS,), jnp.int32).at[e_flat].add(1)
    padded = ((counts + EXPERT_BLOCK - 1) // EXPERT_BLOCK) * EXPERT_BLOCK
    starts = jnp.cumsum(counts) - counts
    padded_ends = jnp.cumsum(padded)
    padded_starts = padded_ends - padded
    rank = jnp.arange(n_assign, dtype=jnp.int32) - starts[e_sorted]
    dest = padded_starts[e_sorted] + rank
    n_blocks = (n_assign + EXPERT_BLOCK - 1) // EXPERT_BLOCK + N_EXPERTS
    n_slots = n_blocks * EXPERT_BLOCK
    slot_tok = jnp.zeros((n_slots,), jnp.int32).at[dest].set(tok_flat[order])
    slot_gate = jnp.zeros((n_slots,), g_flat.dtype).at[dest].set(g_flat[order])
    block_start = jnp.arange(n_blocks, dtype=jnp.int32) * EXPERT_BLOCK
    block_expert = jnp.minimum(jnp.searchsorted(padded_ends, block_start, side='right'),
                               N_EXPERTS - 1).astype(jnp.int32)

    def run_block(args):
        toks, e = args
        xb = h_flat[toks]
        hid = jax.nn.silu(xb @ w_gate_e[e]) * (xb @ w_up_e[e])
        return hid @ w_down_e[e]

    y = lax.map(run_block, (slot_tok.reshape(n_blocks, EXPERT_BLOCK), block_expert))
    y = y.reshape(n_slots, d) * slot_gate[:, None]
    return jax.ops.segment_sum(y, slot_tok, num_segments=n_tok).astype(h_flat.dtype)


def _moe(h, w_router, router_bias, w_gate_e, w_up_e, w_down_e, w_gate_s, w_up_s, w_down_s):
    b, s, d = h.shape
    h_flat = h.reshape(b * s, d)
    idx, w = _route(h_flat, w_router, router_bias)
    routed = _routed_experts(h_flat, idx, w, w_gate_e, w_up_e, w_down_e)
    shared = (jax.nn.silu(h_flat @ w_gate_s) * (h_flat @ w_up_s)) @ w_down_s
    return (routed + shared).reshape(b, s, d)


def setup_inputs(seed: int = 0) -> dict:
    key = jax.random.key(seed)
    ks = jax.random.split(key, 21)
    L, D, E, H, HS = DEPTH, D_MODEL, N_EXPERTS, EXPERT_HIDDEN, SHARED_HIDDEN
    f32 = jnp.float32
    col_scale = jnp.concatenate([
        jnp.full((n,), c, f32) for n, c in zip(
            IN_SPLITS, (1.0, 1.0, DN_BETA, 1.0, 1.0, DN_BETA, 1.0, 1.0))]) * D ** -0.5
    return {
        "x": jax.random.normal(ks[0], (BATCH, SEQ, D), f32),
        "ln_in_g": 1.0 + 0.02 * jax.random.normal(ks[1], (D,), f32),
        "ln_in_b": 0.02 * jax.random.normal(ks[2], (D,), f32),
        "rel_bias": 0.2 * jax.random.normal(ks[3], (REL_BUCKETS, DIL_HEADS), f32),
        "w_in": jax.random.normal(ks[4], (L, D, IN_WIDTH), f32) * col_scale,
        "b_gate": 0.02 * jax.random.normal(ks[5], (L, 2 * D), f32),
        "w_br_sb": jax.random.normal(ks[6], (L, SB_WIDTH, D), f32) * (DN_BETA * SB_WIDTH ** -0.5),
        "w_br_dil": jax.random.normal(ks[7], (L, DIL_OUT_WIDTH, D), f32) * (DN_BETA * DIL_OUT_WIDTH ** -0.5),
        "w_out": jax.random.normal(ks[8], (L, D, D), f32) * (DN_BETA * D ** -0.5),
        "ln1_g": 1.0 + 0.02 * jax.random.normal(ks[9], (L, D), f32),
        "ln1_b": 0.02 * jax.random.normal(ks[10], (L, D), f32),
        "w_router": jax.random.normal(ks[11], (L, D, E), f32) * D ** -0.5,
        "router_bias": 0.01 * jax.random.normal(ks[12], (L, E), f32),
        "w_gate_e": jax.random.normal(ks[13], (L, E, D, H), f32) * (DN_BETA * D ** -0.5),
        "w_up_e": jax.random.normal(ks[14], (L, E, D, H), f32) * (DN_BETA * D ** -0.5),
        "w_down_e": jax.random.normal(ks[15], (L, E, H, D), f32) * (DN_BETA * H ** -0.5),
        "w_gate_s": jax.random.normal(ks[16], (L, D, HS), f32) * (DN_BETA * D ** -0.5),
        "w_up_s": jax.random.normal(ks[17], (L, D, HS), f32) * (DN_BETA * D ** -0.5),
        "w_down_s": jax.random.normal(ks[18], (L, HS, D), f32) * (DN_BETA * HS ** -0.5),
        "ln2_g": 1.0 + 0.02 * jax.random.normal(ks[19], (L, D), f32),
        "ln2_b": 0.02 * jax.random.normal(ks[20], (L, D), f32),
    }


def reference(x, ln_in_g, ln_in_b, rel_bias, w_in, b_gate, w_br_sb, w_br_dil, w_out, ln1_g, ln1_b,
              w_router, router_bias, w_gate_e, w_up_e, w_down_e, w_gate_s, w_up_s, w_down_s,
              ln2_g, ln2_b):
    h = _layer_norm(x, ln_in_g, ln_in_b)
    for l in range(DEPTH):
        mix = _mixer(h, w_in[l], b_gate[l], w_br_sb[l], w_br_dil[l], w_out[l], rel_bias)
        h = _layer_norm(DN_ALPHA * h + mix, ln1_g[l], ln1_b[l])
        ffn = _moe(h, w_router[l], router_bias[l], w_gate_e[l], w_up_e[l], w_down_e[l],
                   w_gate_s[l], w_up_s[l], w_down_s[l])
        h = _layer_norm(DN_ALPHA * h + ffn, ln2_g[l], ln2_b[l])
    return h
```

```python
import functools
import math

import numpy as np
import jax
import jax.numpy as jnp
from jax import lax
from jax.experimental import pallas as pl
from jax.experimental.pallas import tpu as pltpu

F32 = jnp.float32
BF16 = jnp.bfloat16
I32 = jnp.int32

D_MODEL = 1024
HEAD_DIM = 64
SB_HEADS = 8
DIL_PATTERNS = ((128, 1), (512, 4), (2048, 16))
DIL_HEADS_PER_GROUP = 4
DIL_WINDOW_KEYS = 128
REL_BUCKETS = 32
REL_MAX_DISTANCE = 2048
N_EXPERTS = 256
TOP_K = 8
N_GROUPS = 8
GROUP_SIZE = N_EXPERTS // N_GROUPS
TOPK_GROUPS = 4
EXPERT_HIDDEN = 256
ROUTED_SCALE = 2.5
LN_EPS = 1e-5
DEPTH = 1
DN_ALPHA = (2 * DEPTH) ** 0.25

SB_WIDTH = SB_HEADS * HEAD_DIM
DIL_GROUP_WIDTH = DIL_HEADS_PER_GROUP * HEAD_DIM
DIL_WIDTH = DIL_GROUP_WIDTH * len(DIL_PATTERNS)

GATE_COLS = 2 * D_MODEL
COL_SB_Q = GATE_COLS
COL_SB_K = COL_SB_Q + SB_WIDTH
COL_SB_V = COL_SB_K + SB_WIDTH
COL_DL_Q = COL_SB_V + SB_WIDTH
COL_DL_K = COL_DL_Q + DIL_WIDTH
COL_DL_V = COL_DL_K + DIL_WIDTH
PROJ_COLS = COL_DL_V + DIL_WIDTH
PROJ_TN = 2048
PROJ_PAD = -(-PROJ_COLS // PROJ_TN) * PROJ_TN

LANES = 128
EXPERT_BM = 128
MASK_NEG = -1e30
SB_SKIP_LOG = -104.0

VMEM_LIMIT = 48 * 1024 * 1024


def _layer_norm(x, g, b):
    mu = jnp.mean(x, axis=-1, keepdims=True)
    xc = x - mu
    var = jnp.mean(xc * xc, axis=-1, keepdims=True)
    return xc * lax.rsqrt(var + LN_EPS) * g + b


def _dot(a, b):
    return jnp.dot(a, b, preferred_element_type=F32)


def _dot_nt(a, b):
    return lax.dot_general(a, b, (((1,), (1,)), ((), ())), preferred_element_type=F32)


def _silu(x):
    return x * jax.nn.sigmoid(x)


def _inproj_body(x_ref, g_ref, b_ref, w_ref, h_ref, p_ref, hb_ref):
    @pl.when(pl.program_id(1) == 0)
    def _():
        h = _layer_norm(x_ref[...], g_ref[...], b_ref[...])
        h_ref[...] = h
        hb_ref[...] = h.astype(BF16)

    p_ref[...] = _dot(hb_ref[...], w_ref[...]).astype(BF16)


def _inproj(x2, g, b, w_pad):
    t = x2.shape[0]
    tm = 512
    return pl.pallas_call(
        _inproj_body,
        grid=(t // tm, PROJ_PAD // PROJ_TN),
        in_specs=[
            pl.BlockSpec((tm, D_MODEL), lambda i, j: (i, 0)),
            pl.BlockSpec((1, D_MODEL), lambda i, j: (0, 0)),
            pl.BlockSpec((1, D_MODEL), lambda i, j: (0, 0)),
            pl.BlockSpec((D_MODEL, PROJ_TN), lambda i, j: (0, j)),
        ],
        out_specs=[
            pl.BlockSpec((tm, D_MODEL), lambda i, j: (i, 0)),
            pl.BlockSpec((tm, PROJ_TN), lambda i, j: (i, j)),
        ],
        out_shape=[
            jax.ShapeDtypeStruct((t, D_MODEL), F32),
            jax.ShapeDtypeStruct((t, PROJ_PAD), BF16),
        ],
        scratch_shapes=[pltpu.VMEM((tm, D_MODEL), BF16)],
        compiler_params=pltpu.CompilerParams(
            dimension_semantics=("parallel", "arbitrary"), vmem_limit_bytes=VMEM_LIMIT),
        name="ln_inproj",
    )(x2, g, b, w_pad)


SB_TQ = 256


def _sb_body(q_ref, k_ref, v_ref, o_ref, *, tq, scale):
    i = pl.program_id(2)
    q2 = q_ref[0]
    lane = lax.broadcasted_iota(I32, (tq, LANES), 1)
    zero = jnp.zeros_like(q2)
    q_a = jnp.where(lane < HEAD_DIM, q2, zero)
    q_b = jnp.where(lane >= HEAD_DIM, q2, zero)
    row = lax.broadcasted_iota(I32, (tq, tq), 0)
    col = lax.broadcasted_iota(I32, (tq, tq), 1)
    tri = col < row
    later = (row > col).astype(BF16)

    def one_head(qm, carry, acc, kb, vb, before):
        z = _dot_nt(qm, kb) * scale
        softplus = jnp.maximum(z, 0.0) + jnp.log(1.0 + jnp.exp(-jnp.abs(z)))
        log_keep = jnp.where(before, -softplus, 0.0)
        lk_hi = log_keep.astype(BF16)
        lk_lo = (log_keep - lk_hi.astype(F32)).astype(BF16)
        between = _dot(lk_hi, later) + _dot(lk_lo, later)
        total = (z - softplus) + between + carry
        w = jnp.where(before, jnp.exp(total), 0.0)
        acc = acc + _dot(w.astype(BF16), vb)
        carry = carry + jnp.sum(log_keep, axis=-1, keepdims=True)
        return carry, acc

    def cond(st):
        return jnp.logical_and(st[0] >= 0, st[1] > 0)

    def body(st):
        j, _, c_a, c_b, a_a, a_b = st
        ks = pl.multiple_of(j * tq, tq)
        kb = k_ref[0, pl.ds(ks, tq), :]
        vb = v_ref[0, pl.ds(ks, tq), :]
        before = jnp.logical_or(tri, j < i)
        c_a, a_a = one_head(q_a, c_a, a_a, kb, vb, before)
        c_b, a_b = one_head(q_b, c_b, a_b, kb, vb, before)
        go = (jnp.max(jnp.maximum(c_a, c_b)) > SB_SKIP_LOG).astype(I32)
        return j - 1, go, c_a, c_b, a_a, a_b

    c0 = jnp.zeros((tq, 1), F32)
    a0 = jnp.zeros((tq, LANES), F32)
    st = lax.while_loop(cond, body, (i, jnp.int32(1), c0, c0, a0, a0))
    o_ref[0] = jnp.where(lane < HEAD_DIM, st[4], st[5]).astype(BF16)


def _sb_attention(proj3):
    b, s, _ = proj3.shape
    tq = SB_TQ
    qb, kb, vb = COL_SB_Q // LANES, COL_SB_K // LANES, COL_SB_V // LANES
    return pl.pallas_call(
        functools.partial(_sb_body, tq=tq, scale=1.0 / math.sqrt(HEAD_DIM)),
        grid=(b, SB_WIDTH // LANES, s // tq),
        in_specs=[
            pl.BlockSpec((1, tq, LANES), lambda bi, hp, i: (bi, i, qb + hp)),
            pl.BlockSpec((1, s, LANES), lambda bi, hp, i: (bi, 0, kb + hp)),
            pl.BlockSpec((1, s, LANES), lambda bi, hp, i: (bi, 0, vb + hp)),
        ],
        out_specs=pl.BlockSpec((1, tq, LANES), lambda bi, hp, i: (bi, i, hp)),
        out_shape=jax.ShapeDtypeStruct((b, s, SB_WIDTH), BF16),
        compiler_params=pltpu.CompilerParams(
            dimension_semantics=("parallel", "parallel", "arbitrary"), vmem_limit_bytes=VMEM_LIMIT),
        name="sb_attn",
    )(proj3, proj3, proj3)


DIL_TQ = 128


def _rel_bucket_idx(dist):
    max_exact = REL_BUCKETS // 2
    d = jnp.maximum(dist, 1).astype(F32)
    large = max_exact + (jnp.log(d / max_exact) / math.log(REL_MAX_DISTANCE / max_exact)
                         * (REL_BUCKETS - max_exact)).astype(I32)
    large = jnp.minimum(large, REL_BUCKETS - 1)
    return jnp.where(dist < max_exact, dist, large)


def _dil_bias_table(rel_bias, g, window, dilation):
    a = np.arange(DIL_TQ)[:, None]
    c = np.arange(2 * DIL_TQ)[None, :]
    step = a + DIL_TQ - c
    valid = (step >= 0) & (step <= DIL_WINDOW_KEYS)
    step_c = np.clip(step, 0, DIL_WINDOW_KEYS)
    offsets = jnp.arange(window // dilation + 1, dtype=I32) * dilation
    heads = slice(g * DIL_HEADS_PER_GROUP, (g + 1) * DIL_HEADS_PER_GROUP)
    bias = rel_bias[_rel_bucket_idx(offsets)][:, heads].T.astype(F32)
    return jnp.where(valid[None], bias[:, step_c], MASK_NEG)


def _dil_body(q_ref, kp_ref, kc_ref, vp_ref, vc_ref, bias_ref, o_ref, lse_ref, *, tq, scale):
    i = pl.program_id(2)
    q = q_ref[0]
    k2 = jnp.concatenate([kp_ref[0], kc_ref[0]], axis=0)
    v2 = jnp.concatenate([vp_ref[0], vc_ref[0]], axis=0)
    width = DIL_GROUP_WIDTH
    lane = lax.broadcasted_iota(I32, (tq, width), 1)
    keycol = lax.broadcasted_iota(I32, (tq, 2 * tq), 1)
    no_prev = jnp.logical_and(i == 0, keycol < tq)
    zero = jnp.zeros_like(q)
    out = jnp.zeros((tq, width), F32)
    lse_out = jnp.zeros((tq, width), F32)
    for h in range(DIL_HEADS_PER_GROUP):
        in_head = jnp.logical_and(lane >= h * HEAD_DIM, lane < (h + 1) * HEAD_DIM)
        qm = jnp.where(in_head, q, zero)
        s = _dot_nt(qm, k2) * scale + bias_ref[h]
        s = jnp.where(no_prev, MASK_NEG, s)
        m = jnp.max(s, axis=-1, keepdims=True)
        p = jnp.exp(s - m)
        l = jnp.sum(p, axis=-1, keepdims=True)
        pv = _dot(p.astype(BF16), v2)
        out = jnp.where(in_head, pv / l, out)
        lse_out = jnp.where(in_head, m + jnp.log(l), lse_out)
    o_ref[0] = out
    lse_ref[0] = lse_out


def _dilated_group(proj3, bias_tab, g, dilation):
    b, s, npad = proj3.shape
    sd = s // dilation
    tq = DIL_TQ
    width = DIL_GROUP_WIDTH
    pv = proj3.reshape(b, sd, dilation * npad)
    cb = npad // width
    qb, kb, vb = COL_DL_Q // width + g, COL_DL_K // width + g, COL_DL_V // width + g
    blk = (1, tq, width)
    o, lse = pl.pallas_call(
        functools.partial(_dil_body, tq=tq, scale=1.0 / math.sqrt(HEAD_DIM)),
        grid=(b, dilation, sd // tq),
        in_specs=[
            pl.BlockSpec(blk, lambda bi, r, i: (bi, i, r * cb + qb)),
            pl.BlockSpec(blk, lambda bi, r, i: (bi, jnp.maximum(i - 1, 0), r * cb + kb)),
            pl.BlockSpec(blk, lambda bi, r, i: (bi, i, r * cb + kb)),
            pl.BlockSpec(blk, lambda bi, r, i: (bi, jnp.maximum(i - 1, 0), r * cb + vb)),
            pl.BlockSpec(blk, lambda bi, r, i: (bi, i, r * cb + vb)),
            pl.BlockSpec((DIL_HEADS_PER_GROUP, tq, 2 * tq), lambda bi, r, i: (0, 0, 0)),
        ],
        out_specs=[
            pl.BlockSpec(blk, lambda bi, r, i: (bi, i, r)),
            pl.BlockSpec(blk, lambda bi, r, i: (bi, i, r)),
        ],
        out_shape=[
            jax.ShapeDtypeStruct((b, sd, dilation * width), F32),
            jax.ShapeDtypeStruct((b, sd, dilation * width), F32),
        ],
        compiler_params=pltpu.CompilerParams(
            dimension_semantics=("parallel", "parallel", "arbitrary"), vmem_limit_bytes=VMEM_LIMIT),
        name=f"dil_attn_g{g}",
    )(pv, pv, pv, pv, pv, bias_tab)
    return o.reshape(b * s, width), lse.reshape(b * s, width)


MERGE_TM = 256


def _merge_body(gp_ref, osb_ref, o0_ref, o1_ref, o2_ref, l0_ref, l1_ref, l2_ref, h_ref, bg_ref,
                wsb_ref, wdl_ref, wout_ref, g1_ref, b1_ref, wrh_ref, wrl_ref, h1_ref, sc_ref):
    l0, l1, l2 = l0_ref[...], l1_ref[...], l2_ref[...]
    m = jnp.maximum(jnp.maximum(l0, l1), l2)
    e0, e1, e2 = jnp.exp(l0 - m), jnp.exp(l1 - m), jnp.exp(l2 - m)
    o_dl = (e0 * o0_ref[...] + e1 * o1_ref[...] + e2 * o2_ref[...]) / (e0 + e1 + e2)
    br_sb = _dot(osb_ref[...], wsb_ref[...])
    br_dl = _dot(o_dl.astype(BF16), wdl_ref[...])
    gates = jax.nn.sigmoid(gp_ref[...].astype(F32) + bg_ref[...])
    merged = gates[:, :D_MODEL] * br_sb + gates[:, D_MODEL:] * br_dl
    mix = _dot(merged.astype(BF16), wout_ref[...])
    h1 = _layer_norm(DN_ALPHA * h_ref[...] + mix, g1_ref[...], b1_ref[...])
    h1_ref[...] = h1
    h_hi = h1.astype(BF16)
    h_lo = (h1 - h_hi.astype(F32)).astype(BF16)
    logits = _dot(h_hi, wrh_ref[...]) + _dot(h_lo, wrh_ref[...]) + _dot(h_hi, wrl_ref[...])
    sc_ref[...] = jax.nn.sigmoid(logits)


def _merge(proj, o_sb, dl, h, b_gate, w_br_sb, w_br_dil, w_out, g1, b1, wr_hi, wr_lo):
    t = h.shape[0]
    tm = MERGE_TM
    row = lambda w: pl.BlockSpec((tm, w), lambda i: (i, 0))
    full = lambda a: pl.BlockSpec(a.shape, lambda i: (0,) * a.ndim)
    args = (proj, o_sb, dl[0][0], dl[1][0], dl[2][0], dl[0][1], dl[1][1], dl[2][1], h, b_gate,
            w_br_sb, w_br_dil, w_out, g1, b1, wr_hi, wr_lo)
    in_specs = [row(GATE_COLS), row(SB_WIDTH)] + [row(DIL_GROUP_WIDTH)] * 6 + [row(D_MODEL)] \
        + [full(a) for a in args[9:]]
    return pl.pallas_call(
        _merge_body,
        grid=(t // tm,),
        in_specs=in_specs,
        out_specs=[row(D_MODEL), row(N_EXPERTS)],
        out_shape=[jax.ShapeDtypeStruct((t, D_MODEL), F32), jax.ShapeDtypeStruct((t, N_EXPERTS), F32)],
        compiler_params=pltpu.CompilerParams(
            dimension_semantics=("parallel",), vmem_limit_bytes=VMEM_LIMIT),
        name="merge_ln1",
    )(*args)


ROUTE_TR = 256


def _route_body(sc_ref, rb_ref, idx_ref, w_ref, rank_ref, cnt_ref, carry_ref, *, tr):
    @pl.when(pl.program_id(0) == 0)
    def _():
        carry_ref[...] = jnp.zeros_like(carry_ref)

    scores = sc_ref[...]
    biased = scores + rb_ref[...]
    lane = lax.broadcasted_iota(I32, (tr, N_EXPERTS), 1)
    grp = lane // GROUP_SIZE
    neg = -jnp.inf

    def first_argmax(v):
        m = jnp.max(v, axis=-1, keepdims=True)
        return m, jnp.min(jnp.where(v == m, lane, N_EXPERTS), axis=-1, keepdims=True)

    gscore = []
    for g in range(N_GROUPS):
        vg = jnp.where(grp == g, biased, neg)
        m1, i1 = first_argmax(vg)
        m2 = jnp.max(jnp.where(lane == i1, neg, vg), axis=-1, keepdims=True)
        gscore.append(m1 + m2)
    emask = jnp.zeros((tr, N_EXPERTS), jnp.bool_)
    for g in range(N_GROUPS):
        beaten = jnp.zeros((tr, 1), I32)
        for g2 in range(N_GROUPS):
            if g2 == g:
                continue
            wins = gscore[g2] > gscore[g]
            if g2 < g:
                wins = jnp.logical_or(wins, gscore[g2] == gscore[g])
            beaten = beaten + wins.astype(I32)
        emask = jnp.logical_or(emask, jnp.logical_and(grp == g, beaten < TOPK_GROUPS))
    masked = jnp.where(emask, biased, neg)
    sel = jnp.zeros((tr, N_EXPERTS), jnp.bool_)
    picks = []
    for _ in range(TOP_K):
        _, ik = first_argmax(masked)
        hit = lane == ik
        sel = jnp.logical_or(sel, hit)
        masked = jnp.where(hit, neg, masked)
        picks.append(ik)
    denom = jnp.sum(jnp.where(sel, scores, 0.0), axis=-1, keepdims=True)

    r = lax.broadcasted_iota(I32, (tr, tr), 0)
    c = lax.broadcasted_iota(I32, (tr, tr), 1)
    lower = (c < r).astype(BF16)
    sel_f = sel.astype(F32)
    before = _dot(lower, sel_f.astype(BF16)) + carry_ref[...]
    new_carry = carry_ref[...] + jnp.sum(sel_f, axis=0, keepdims=True)
    carry_ref[...] = new_carry
    cnt_ref[...] = new_carry

    lane_o = lax.broadcasted_iota(I32, (tr, LANES), 1)
    idx_o = jnp.zeros((tr, LANES), I32)
    rank_o = jnp.zeros((tr, LANES), I32)
    w_o = jnp.zeros((tr, LANES), F32)
    for k, ik in enumerate(picks):
        hit = lane == ik
        wk = jnp.sum(jnp.where(hit, scores, 0.0), axis=-1, keepdims=True) / denom * ROUTED_SCALE
        rk = jnp.sum(jnp.where(hit, before, 0.0), axis=-1, keepdims=True).astype(I32)
        idx_o = jnp.where(lane_o == k, ik, idx_o)
        rank_o = jnp.where(lane_o == k, rk, rank_o)
        w_o = jnp.where(lane_o == k, wk, w_o)
    idx_ref[...] = idx_o
    rank_ref[...] = rank_o
    w_ref[...] = w_o


def _route(scores, router_bias):
    t = scores.shape[0]
    tr = ROUTE_TR
    row = lambda w: pl.BlockSpec((tr, w), lambda i: (i, 0))
    one = pl.BlockSpec((1, N_EXPERTS), lambda i: (0, 0))
    return pl.pallas_call(
        functools.partial(_route_body, tr=tr),
        grid=(t // tr,),
        in_specs=[row(N_EXPERTS), one],
        out_specs=[row(LANES), row(LANES), row(LANES), one],
        out_shape=[jax.ShapeDtypeStruct((t, LANES), I32), jax.ShapeDtypeStruct((t, LANES), F32),
                   jax.ShapeDtypeStruct((t, LANES), I32), jax.ShapeDtypeStruct((1, N_EXPERTS), F32)],
        scratch_shapes=[pltpu.VMEM((1, N_EXPERTS), F32)],
        compiler_params=pltpu.CompilerParams(
            dimension_semantics=("arbitrary",), vmem_limit_bytes=VMEM_LIMIT),
        name="route",
    )(scores, router_bias)


DISPATCH_TD = 128


def _row_copy(src_ref, src_row, dst_ref, dst_row, sem):
    return pltpu.make_async_copy(src_ref.at[pl.ds(src_row, 1), :], dst_ref.at[pl.ds(dst_row, 1), :], sem)


def _dispatch_body(ps_ref, idx_ref, rank_ref, x_ref, zeros_ref, xs_ref, sem, *, td):
    del zeros_ref

    def issue(i, carry):
        for k in range(TOP_K):
            slot = ps_ref[idx_ref[i * TOP_K + k]] + rank_ref[i * TOP_K + k]
            _row_copy(x_ref, i, xs_ref, slot, sem).start()
        return carry

    lax.fori_loop(0, td, issue, 0)

    def drain(i, carry):
        for k in range(TOP_K):
            _row_copy(x_ref, 0, xs_ref, 0, sem).wait()
        return carry

    lax.fori_loop(0, td, drain, 0)


def _dispatch(pstarts, idx_flat, rank_flat, h1, n_slots):
    t = h1.shape[0]
    td = DISPATCH_TD
    zeros = jnp.zeros((n_slots, D_MODEL), F32)
    smem = lambda: pl.BlockSpec((td * TOP_K,), lambda i, ps: (i,), memory_space=pltpu.SMEM)
    return pl.pallas_call(
        functools.partial(_dispatch_body, td=td),
        grid_spec=pltpu.PrefetchScalarGridSpec(
            num_scalar_prefetch=1,
            grid=(t // td,),
            in_specs=[smem(), smem(),
                      pl.BlockSpec((td, D_MODEL), lambda i, ps: (i, 0)),
                      pl.BlockSpec(memory_space=pl.ANY)],
            out_specs=pl.BlockSpec(memory_space=pl.ANY),
            scratch_shapes=[pltpu.SemaphoreType.DMA],
        ),
        out_shape=jax.ShapeDtypeStruct((n_slots, D_MODEL), F32),
        input_output_aliases={4: 0},
        compiler_params=pltpu.CompilerParams(
            dimension_semantics=("arbitrary",), vmem_limit_bytes=VMEM_LIMIT),
        name="dispatch",
    )(pstarts, idx_flat, rank_flat, h1, zeros)


def _expert_body(be_ref, xs_ref, wg_ref, wu_ref, wd_ref, y_ref):
    del be_ref
    x = xs_ref[...].astype(BF16)
    gate = _dot(x, wg_ref[0].astype(BF16))
    up = _dot(x, wu_ref[0].astype(BF16))
    hid = _silu(gate) * up
    y_ref[...] = _dot(hid.astype(BF16), wd_ref[0].astype(BF16))


def _experts(block_expert, xs, w_gate_e, w_up_e, w_down_e):
    n_slots = xs.shape[0]
    bm = EXPERT_BM
    return pl.pallas_call(
        _expert_body,
        grid_spec=pltpu.PrefetchScalarGridSpec(
            num_scalar_prefetch=1,
            grid=(n_slots // bm,),
            in_specs=[
                pl.BlockSpec((bm, D_MODEL), lambda i, be: (i, 0)),
                pl.BlockSpec((1, D_MODEL, EXPERT_HIDDEN), lambda i, be: (be[i], 0, 0)),
                pl.BlockSpec((1, D_MODEL, EXPERT_HIDDEN), lambda i, be: (be[i], 0, 0)),
                pl.BlockSpec((1, EXPERT_HIDDEN, D_MODEL), lambda i, be: (be[i], 0, 0)),
            ],
            out_specs=pl.BlockSpec((bm, D_MODEL), lambda i, be: (i, 0)),
        ),
        out_shape=jax.ShapeDtypeStruct((n_slots, D_MODEL), F32),
        compiler_params=pltpu.CompilerParams(
            dimension_semantics=("arbitrary",), vmem_limit_bytes=VMEM_LIMIT),
        name="experts",
    )(block_expert, xs, w_gate_e, w_up_e, w_down_e)


COMBINE_TC = 128


def _combine_body(ps_ref, idx_ref, rank_ref, w_ref, h1_ref, ys_ref, wgs_ref, wus_ref, wds_ref,
                  g2_ref, b2_ref, o_ref, buf, sem, *, tc):
    def issue(i, carry):
        for k in range(TOP_K):
            slot = ps_ref[idx_ref[i * TOP_K + k]] + rank_ref[i * TOP_K + k]
            _row_copy(ys_ref, slot, buf.at[k], i, sem).start()
        return carry

    lax.fori_loop(0, tc, issue, 0)

    h1 = h1_ref[...]
    hb = h1.astype(BF16)
    shared = _dot((_silu(_dot(hb, wgs_ref[...])) * _dot(hb, wus_ref[...])).astype(BF16), wds_ref[...])

    def drain(i, carry):
        for k in range(TOP_K):
            _row_copy(ys_ref, 0, buf.at[k], 0, sem).wait()
        return carry

    lax.fori_loop(0, tc, drain, 0)

    w = w_ref[...]
    routed = jnp.zeros((tc, D_MODEL), F32)
    for k in range(TOP_K):
        routed = routed + w[:, k:k + 1] * buf[k]
    o_ref[...] = _layer_norm(DN_ALPHA * h1 + routed + shared, g2_ref[...], b2_ref[...])


def _combine(pstarts, idx_flat, rank_flat, w_pad, h1, ys, wgs, wus, wds, g2, b2):
    t = h1.shape[0]
    tc = COMBINE_TC
    smem = lambda: pl.BlockSpec((tc * TOP_K,), lambda i, ps: (i,), memory_space=pltpu.SMEM)
    full = lambda a: pl.BlockSpec(a.shape, lambda i, ps: (0,) * a.ndim)
    return pl.pallas_call(
        functools.partial(_combine_body, tc=tc),
        grid_spec=pltpu.PrefetchScalarGridSpec(
            num_scalar_prefetch=1,
            grid=(t // tc,),
            in_specs=[smem(), smem(),
                      pl.BlockSpec((tc, LANES), lambda i, ps: (i, 0)),
                      pl.BlockSpec((tc, D_MODEL), lambda i, ps: (i, 0)),
                      pl.BlockSpec(memory_space=pl.ANY),
                      full(wgs), full(wus), full(wds), full(g2), full(b2)],
            out_specs=pl.BlockSpec((tc, D_MODEL), lambda i, ps: (i, 0)),
            scratch_shapes=[pltpu.VMEM((TOP_K, tc, D_MODEL), F32), pltpu.SemaphoreType.DMA],
        ),
        out_shape=jax.ShapeDtypeStruct((t, D_MODEL), F32),
        compiler_params=pltpu.CompilerParams(
            dimension_semantics=("arbitrary",), vmem_limit_bytes=VMEM_LIMIT),
        name="combine_ln2",
    )(pstarts, idx_flat, rank_flat, w_pad, h1, ys, wgs, wus, wds, g2, b2)


def _regroup_w_in(w):
    qkv = w[:, :PROJ_COLS - GATE_COLS]
    gates = w[:, PROJ_COLS - GATE_COLS:]
    pad = jnp.zeros((w.shape[0], PROJ_PAD - PROJ_COLS), w.dtype)
    return jnp.concatenate([gates, qkv, pad], axis=1).astype(BF16)


def _moe(h1, scores, router_bias, w_gate_e, w_up_e, w_down_e, w_gate_s, w_up_s, w_down_s, g2, b2):
    t = h1.shape[0]
    idx_pad, w_pad, rank_pad, counts = _route(scores, router_bias[None])
    counts = counts[0].astype(I32)
    padded = (counts + EXPERT_BM - 1) // EXPERT_BM * EXPERT_BM
    pends = jnp.cumsum(padded)
    pstarts = (pends - padded).astype(I32)
    n_blocks = (t * TOP_K) // EXPERT_BM + N_EXPERTS
    block_start = jnp.arange(n_blocks, dtype=I32) * EXPERT_BM
    block_expert = jnp.minimum(jnp.searchsorted(pends, block_start, side="right"),
                               N_EXPERTS - 1).astype(I32)
    idx_flat = idx_pad[:, :TOP_K].reshape(-1)
    rank_flat = rank_pad[:, :TOP_K].reshape(-1)
    xs = _dispatch(pstarts, idx_flat, rank_flat, h1, n_blocks * EXPERT_BM)
    ys = _experts(block_expert, xs, w_gate_e, w_up_e, w_down_e)
    return _combine(pstarts, idx_flat, rank_flat, w_pad, h1, ys,
                    w_gate_s.astype(BF16), w_up_s.astype(BF16), w_down_s.astype(BF16),
                    g2[None], b2[None])


def kernel(x, ln_in_g, ln_in_b, rel_bias, w_in, b_gate, w_br_sb, w_br_dil, w_out, ln1_g, ln1_b,
           w_router, router_bias, w_gate_e, w_up_e, w_down_e, w_gate_s, w_up_s, w_down_s,
           ln2_g, ln2_b):
    b, s, d = x.shape
    t = b * s
    n_layers = w_in.shape[0]
    h = x.reshape(t, d)
    ln_g, ln_b = ln_in_g[None], ln_in_b[None]
    bias_tabs = [_dil_bias_table(rel_bias, g, window, dil) for g, (window, dil) in enumerate(DIL_PATTERNS)]
    for l in range(n_layers):
        assert l == 0, "single-layer block"
        h, proj = _inproj(h, ln_g, ln_b, _regroup_w_in(w_in[l]))
        proj3 = proj.reshape(b, s, PROJ_PAD)
        o_sb = _sb_attention(proj3).reshape(t, SB_WIDTH)
        dl = [_dilated_group(proj3, bias_tabs[g], g, dil) for g, (_, dil) in enumerate(DIL_PATTERNS)]
        wr = w_router[l]
        wr_hi = wr.astype(BF16)
        wr_lo = (wr - wr_hi.astype(F32)).astype(BF16)
        h1, scores = _merge(proj, o_sb, dl, h, b_gate[l][None], w_br_sb[l].astype(BF16),
                            w_br_dil[l].astype(BF16), w_out[l].astype(BF16),
                            ln1_g[l][None], ln1_b[l][None], wr_hi, wr_lo)
        h = _moe(h1, scores, router_bias[l], w_gate_e[l], w_up_e[l], w_down_e[l],
                 w_gate_s[l], w_up_s[l], w_down_s[l], ln2_g[l], ln2_b[l])
    return h.reshape(b, s, d)
```

```python
import functools
import math

import numpy as np
import jax
import jax.numpy as jnp
from jax import lax
from jax.experimental import pallas as pl
from jax.experimental.pallas import tpu as pltpu

F32 = jnp.float32
BF16 = jnp.bfloat16
I32 = jnp.int32

D_MODEL = 1024
HEAD_DIM = 64
SB_HEADS = 8
DIL_PATTERNS = ((128, 1), (512, 4), (2048, 16))
DIL_HEADS_PER_GROUP = 4
DIL_WINDOW_KEYS = 128
REL_BUCKETS = 32
REL_MAX_DISTANCE = 2048
N_EXPERTS = 256
TOP_K = 8
N_GROUPS = 8
GROUP_SIZE = N_EXPERTS // N_GROUPS
TOPK_GROUPS = 4
EXPERT_HIDDEN = 256
ROUTED_SCALE = 2.5
LN_EPS = 1e-5
DEPTH = 1
DN_ALPHA = (2 * DEPTH) ** 0.25

SB_WIDTH = SB_HEADS * HEAD_DIM
DIL_GROUP_WIDTH = DIL_HEADS_PER_GROUP * HEAD_DIM
DIL_WIDTH = DIL_GROUP_WIDTH * len(DIL_PATTERNS)

GATE_COLS = 2 * D_MODEL
SB_COLS = 3 * SB_WIDTH
DIL_GROUP_COLS = 3 * DIL_GROUP_WIDTH
N_DIL = len(DIL_PATTERNS)

LANES = 128
SUBLANES = 8
EXPERT_BM = 256
MASK_NEG = -1e30
SB_SKIP_LOG = -104.0

VMEM_LIMIT = 48 * 1024 * 1024


def _layer_norm(x, g, b):
    mu = jnp.mean(x, axis=-1, keepdims=True)
    xc = x - mu
    var = jnp.mean(xc * xc, axis=-1, keepdims=True)
    return xc * lax.rsqrt(var + LN_EPS) * g + b


def _dot(a, b):
    return jnp.dot(a, b, preferred_element_type=F32)


def _dot_nt(a, b):
    return lax.dot_general(a, b, (((1,), (1,)), ((), ())), preferred_element_type=F32)


def _silu(x):
    return x * jax.nn.sigmoid(x)


INPROJ_TM = 512


def _inproj_body(x_ref, g_ref, b_ref, wg_ref, wsb_ref, wdl_ref, h_ref, gate_ref, sb_ref, *rest):
    dl_refs, slab_ref = rest[:N_DIL], rest[N_DIL]
    tm = x_ref.shape[0]
    h = _layer_norm(x_ref[...], g_ref[...], b_ref[...])
    h_ref[...] = h
    hb = h.astype(BF16)
    gate_ref[...] = _dot(hb, wg_ref[...]).astype(BF16)
    sb_ref[...] = _dot(hb, wsb_ref[...]).astype(BF16)
    dl = _dot(hb, wdl_ref[...])
    slabs_per_group = DIL_GROUP_COLS // LANES
    for g, (_, dilation) in enumerate(DIL_PATTERNS):
        cols = dl[:, g * DIL_GROUP_COLS:(g + 1) * DIL_GROUP_COLS]
        if dilation == 1:
            dl_refs[g][0, 0] = cols.astype(BF16)
            continue
        for s in range(slabs_per_group):
            slab_ref[s] = cols[:, s * LANES:(s + 1) * LANES]
        for r in range(dilation):
            for s in range(slabs_per_group):
                rows = slab_ref[s, pl.ds(r, tm // dilation, stride=dilation), :]
                dl_refs[g][0, r, :, s * LANES:(s + 1) * LANES] = rows.astype(BF16)


def _inproj(x2, g, b, w_gate, w_sb, w_dl, batch):
    t = x2.shape[0]
    tm = INPROJ_TM
    s = t // batch
    tiles_per_seq = s // tm
    const = lambda a: pl.BlockSpec(a.shape, lambda i: (0,) * a.ndim, pipeline_mode=pl.Buffered(1))
    row = lambda w: pl.BlockSpec((tm, w), lambda i: (i, 0))
    dl_specs, dl_shapes = [], []
    for _, dilation in DIL_PATTERNS:
        dl_specs.append(pl.BlockSpec((1, dilation, tm // dilation, DIL_GROUP_COLS),
                                     lambda i: (i // tiles_per_seq, 0, i % tiles_per_seq, 0)))
        dl_shapes.append(jax.ShapeDtypeStruct((batch, dilation, s // dilation, DIL_GROUP_COLS), BF16))
    return pl.pallas_call(
        _inproj_body,
        grid=(t // tm,),
        in_specs=[row(D_MODEL), const(g), const(b), const(w_gate), const(w_sb), const(w_dl)],
        out_specs=[row(D_MODEL), row(GATE_COLS), row(SB_COLS)] + dl_specs,
        out_shape=[jax.ShapeDtypeStruct((t, D_MODEL), F32),
                   jax.ShapeDtypeStruct((t, GATE_COLS), BF16),
                   jax.ShapeDtypeStruct((t, SB_COLS), BF16)] + dl_shapes,
        scratch_shapes=[pltpu.VMEM((DIL_GROUP_COLS // LANES, tm, LANES), F32)],
        compiler_params=pltpu.CompilerParams(
            dimension_semantics=("parallel",), vmem_limit_bytes=VMEM_LIMIT),
        name="ln_inproj",
    )(x2, g, b, w_gate, w_sb, w_dl)


SB_TQ = 256


def _sb_body(q_ref, k_ref, v_ref, o_ref, *, tq, scale):
    i = pl.program_id(2)
    q2 = q_ref[0]
    lane = lax.broadcasted_iota(I32, (tq, LANES), 1)
    zero = jnp.zeros_like(q2)
    q_a = jnp.where(lane < HEAD_DIM, q2, zero)
    q_b = jnp.where(lane >= HEAD_DIM, q2, zero)
    row = lax.broadcasted_iota(I32, (tq, tq), 0)
    col = lax.broadcasted_iota(I32, (tq, tq), 1)
    tri = col < row
    later = (row > col).astype(BF16)

    def one_head(qm, carry, acc, kb, vb, before):
        z = _dot_nt(qm, kb) * scale
        softplus = jnp.maximum(z, 0.0) + jnp.log(1.0 + jnp.exp(-jnp.abs(z)))
        log_keep = jnp.where(before, -softplus, 0.0)
        lk_hi = log_keep.astype(BF16)
        lk_lo = (log_keep - lk_hi.astype(F32)).astype(BF16)
        between = _dot(lk_hi, later) + _dot(lk_lo, later)
        total = (z - softplus) + between + carry
        w = jnp.where(before, jnp.exp(total), 0.0)
        acc = acc + _dot(w.astype(BF16), vb)
        carry = carry + jnp.sum(log_keep, axis=-1, keepdims=True)
        return carry, acc

    def cond(st):
        return jnp.logical_and(st[0] >= 0, st[1] > 0)

    def body(st):
        j, _, c_a, c_b, a_a, a_b = st
        ks = pl.multiple_of(j * tq, tq)
        kb = k_ref[0, pl.ds(ks, tq), :]
        vb = v_ref[0, pl.ds(ks, tq), :]
        before = jnp.logical_or(tri, j < i)
        c_a, a_a = one_head(q_a, c_a, a_a, kb, vb, before)
        c_b, a_b = one_head(q_b, c_b, a_b, kb, vb, before)
        go = (jnp.max(jnp.maximum(c_a, c_b)) > SB_SKIP_LOG).astype(I32)
        return j - 1, go, c_a, c_b, a_a, a_b

    c0 = jnp.zeros((tq, 1), F32)
    a0 = jnp.zeros((tq, LANES), F32)
    st = lax.while_loop(cond, body, (i, jnp.int32(1), c0, c0, a0, a0))
    o_ref[0] = jnp.where(lane < HEAD_DIM, st[4], st[5]).astype(BF16)


def _sb_attention(proj3):
    b, s, _ = proj3.shape
    tq = SB_TQ
    qb, kb, vb = 0, SB_WIDTH // LANES, 2 * SB_WIDTH // LANES
    return pl.pallas_call(
        functools.partial(_sb_body, tq=tq, scale=1.0 / math.sqrt(HEAD_DIM)),
        grid=(b, SB_WIDTH // LANES, s // tq),
        in_specs=[
            pl.BlockSpec((1, tq, LANES), lambda bi, hp, i: (bi, i, qb + hp)),
            pl.BlockSpec((1, s, LANES), lambda bi, hp, i: (bi, 0, kb + hp)),
            pl.BlockSpec((1, s, LANES), lambda bi, hp, i: (bi, 0, vb + hp)),
        ],
        out_specs=pl.BlockSpec((1, tq, LANES), lambda bi, hp, i: (bi, i, hp)),
        out_shape=jax.ShapeDtypeStruct((b, s, SB_WIDTH), BF16),
        compiler_params=pltpu.CompilerParams(
            dimension_semantics=("parallel", "parallel", "arbitrary"), vmem_limit_bytes=VMEM_LIMIT),
        name="sb_attn",
    )(proj3, proj3, proj3)


DIL_TQ = 128


def _rel_bucket_idx(dist):
    max_exact = REL_BUCKETS // 2
    d = jnp.maximum(dist, 1).astype(F32)
    large = max_exact + (jnp.log(d / max_exact) / math.log(REL_MAX_DISTANCE / max_exact)
                         * (REL_BUCKETS - max_exact)).astype(I32)
    large = jnp.minimum(large, REL_BUCKETS - 1)
    return jnp.where(dist < max_exact, dist, large)


def _dil_bias_table(rel_bias, g, window, dilation):
    del window
    a = np.arange(DIL_TQ)[:, None]
    c = np.arange(2 * DIL_TQ)[None, :]
    step = a + DIL_TQ - c
    valid = (step >= 0) & (step <= DIL_WINDOW_KEYS)
    bucket = _rel_bucket_idx(jnp.asarray(np.clip(step, 0, DIL_WINDOW_KEYS) * dilation, I32))
    heads = slice(g * DIL_HEADS_PER_GROUP, (g + 1) * DIL_HEADS_PER_GROUP)
    rb = rel_bias[:, heads].astype(F32)
    tab = jnp.zeros((DIL_HEADS_PER_GROUP,) + step.shape, F32)
    for bkt in range(REL_BUCKETS):
        tab = jnp.where(bucket[None] == bkt, rb[bkt][:, None, None], tab)
    return jnp.where(valid[None], tab, MASK_NEG)


def _dil_body(q_ref, kp_ref, kc_ref, vp_ref, vc_ref, bias_ref, o_ref, lse_ref, *, tq, scale):
    i = pl.program_id(2)
    q = q_ref[0]
    k2 = jnp.concatenate([kp_ref[0], kc_ref[0]], axis=0)
    v2 = jnp.concatenate([vp_ref[0], vc_ref[0]], axis=0)
    width = DIL_GROUP_WIDTH
    lane = lax.broadcasted_iota(I32, (tq, width), 1)
    keycol = lax.broadcasted_iota(I32, (tq, 2 * tq), 1)
    no_prev = jnp.logical_and(i == 0, keycol < tq)
    zero = jnp.zeros_like(q)
    out = jnp.zeros((tq, width), F32)
    lse_out = jnp.zeros((tq, width), F32)
    for h in range(DIL_HEADS_PER_GROUP):
        in_head = jnp.logical_and(lane >= h * HEAD_DIM, lane < (h + 1) * HEAD_DIM)
        qm = jnp.where(in_head, q, zero)
        s = _dot_nt(qm, k2) * scale + bias_ref[h]
        s = jnp.where(no_prev, MASK_NEG, s)
        m = jnp.max(s, axis=-1, keepdims=True)
        p = jnp.exp(s - m)
        l = jnp.sum(p, axis=-1, keepdims=True)
        pv = _dot(p.astype(BF16), v2)
        out = jnp.where(in_head, pv / l, out)
        lse_out = jnp.where(in_head, m + jnp.log(l), lse_out)
    o_ref[0] = out
    lse_ref[0] = lse_out


def _dilated_group(qkv, bias_tab, g):
    b, dilation, sd, _ = qkv.shape
    tq = DIL_TQ
    width = DIL_GROUP_WIDTH
    blk = (1, None, tq, width)
    o, lse = pl.pallas_call(
        functools.partial(_dil_body, tq=tq, scale=1.0 / math.sqrt(HEAD_DIM)),
        grid=(b, dilation, sd // tq),
        in_specs=[
            pl.BlockSpec(blk, lambda bi, r, i: (bi, r, i, 0)),
            pl.BlockSpec(blk, lambda bi, r, i: (bi, r, jnp.maximum(i - 1, 0), 1)),
            pl.BlockSpec(blk, lambda bi, r, i: (bi, r, i, 1)),
            pl.BlockSpec(blk, lambda bi, r, i: (bi, r, jnp.maximum(i - 1, 0), 2)),
            pl.BlockSpec(blk, lambda bi, r, i: (bi, r, i, 2)),
            pl.BlockSpec((DIL_HEADS_PER_GROUP, tq, 2 * tq), lambda bi, r, i: (0, 0, 0)),
        ],
        out_specs=[
            pl.BlockSpec((1, tq, width), lambda bi, r, i: (bi, i, r)),
            pl.BlockSpec((1, tq, width), lambda bi, r, i: (bi, i, r)),
        ],
        out_shape=[
            jax.ShapeDtypeStruct((b, sd, dilation * width), F32),
            jax.ShapeDtypeStruct((b, sd, dilation * width), F32),
        ],
        compiler_params=pltpu.CompilerParams(
            dimension_semantics=("parallel", "parallel", "arbitrary"), vmem_limit_bytes=VMEM_LIMIT),
        name=f"dil_attn_g{g}",
    )(qkv, qkv, qkv, qkv, qkv, bias_tab)
    t = b * dilation * sd
    return o.reshape(t, width), lse.reshape(t, width)


MERGE_TM = 256


def _merge_body(gp_ref, osb_ref, o0_ref, o1_ref, o2_ref, l0_ref, l1_ref, l2_ref, h_ref, bg_ref,
                wsb_ref, wdl_ref, wout_ref, g1_ref, b1_ref, wrh_ref, wrl_ref, h1_ref, sc_ref):
    l0, l1, l2 = l0_ref[...], l1_ref[...], l2_ref[...]
    m = jnp.maximum(jnp.maximum(l0, l1), l2)
    e0, e1, e2 = jnp.exp(l0 - m), jnp.exp(l1 - m), jnp.exp(l2 - m)
    o_dl = (e0 * o0_ref[...] + e1 * o1_ref[...] + e2 * o2_ref[...]) / (e0 + e1 + e2)
    br_sb = _dot(osb_ref[...], wsb_ref[...])
    br_dl = _dot(o_dl.astype(BF16), wdl_ref[...])
    gates = jax.nn.sigmoid(gp_ref[...].astype(F32) + bg_ref[...])
    merged = gates[:, :D_MODEL] * br_sb + gates[:, D_MODEL:] * br_dl
    mix = _dot(merged.astype(BF16), wout_ref[...])
    h1 = _layer_norm(DN_ALPHA * h_ref[...] + mix, g1_ref[...], b1_ref[...])
    h1_ref[...] = h1
    h_hi = h1.astype(BF16)
    h_lo = (h1 - h_hi.astype(F32)).astype(BF16)
    logits = _dot_nt(wrh_ref[...], h_hi) + _dot_nt(wrh_ref[...], h_lo) + _dot_nt(wrl_ref[...], h_hi)
    sc_ref[...] = jax.nn.sigmoid(logits)


def _merge(gates, o_sb, dl, h, b_gate, w_br_sb, w_br_dil, w_out, g1, b1, wr_hi_t, wr_lo_t):
    t = h.shape[0]
    tm = MERGE_TM
    row = lambda w: pl.BlockSpec((tm, w), lambda i: (i, 0))
    full = lambda a: pl.BlockSpec(a.shape, lambda i: (0,) * a.ndim)
    args = (gates, o_sb, dl[0][0], dl[1][0], dl[2][0], dl[0][1], dl[1][1], dl[2][1], h, b_gate,
            w_br_sb, w_br_dil, w_out, g1, b1, wr_hi_t, wr_lo_t)
    in_specs = [row(GATE_COLS), row(SB_WIDTH)] + [row(DIL_GROUP_WIDTH)] * 6 + [row(D_MODEL)] \
        + [full(a) for a in args[9:]]
    return pl.pallas_call(
        _merge_body,
        grid=(t // tm,),
        in_specs=in_specs,
        out_specs=[row(D_MODEL), pl.BlockSpec((N_EXPERTS, tm), lambda i: (0, i))],
        out_shape=[jax.ShapeDtypeStruct((t, D_MODEL), F32), jax.ShapeDtypeStruct((N_EXPERTS, t), F32)],
        compiler_params=pltpu.CompilerParams(
            dimension_semantics=("parallel",), vmem_limit_bytes=VMEM_LIMIT),
        name="merge_ln1",
    )(*args)


ROUTE_TR = 256


def _route_body(sc_ref, rb_ref, idx_ref, w_ref, rank_ref, cnt_ref, carry_ref, *, tr):
    @pl.when(pl.program_id(0) == 0)
    def _():
        carry_ref[...] = jnp.zeros_like(carry_ref)

    scores = sc_ref[...]
    biased = scores + rb_ref[...]
    erow = lax.broadcasted_iota(I32, (N_EXPERTS, tr), 0)
    neg = -jnp.inf

    def first_argmax(v, rows):
        m = jnp.max(v, axis=0, keepdims=True)
        return m, jnp.min(jnp.where(v == m, rows, N_EXPERTS), axis=0, keepdims=True)

    gslice = lambda a, g: a[g * GROUP_SIZE:(g + 1) * GROUP_SIZE, :]
    gscore = []
    grow = lax.broadcasted_iota(I32, (GROUP_SIZE, tr), 0)
    for g in range(N_GROUPS):
        vg, rg = gslice(biased, g), grow + g * GROUP_SIZE
        m1, i1 = first_argmax(vg, rg)
        m2 = jnp.max(jnp.where(rg == i1, neg, vg), axis=0, keepdims=True)
        gscore.append(m1 + m2)
    pieces = []
    for g in range(N_GROUPS):
        beaten = jnp.zeros((1, tr), I32)
        for g2 in range(N_GROUPS):
            if g2 == g:
                continue
            wins = gscore[g2] > gscore[g]
            if g2 < g:
                wins = jnp.logical_or(wins, gscore[g2] == gscore[g])
            beaten = beaten + wins.astype(I32)
        pieces.append(jnp.where(beaten < TOPK_GROUPS, gslice(biased, g), neg))
    masked = jnp.concatenate(pieces, axis=0)
    sel = jnp.zeros((N_EXPERTS, tr), jnp.bool_)
    picks = []
    for _ in range(TOP_K):
        _, ik = first_argmax(masked, erow)
        hit = erow == ik
        sel = jnp.logical_or(sel, hit)
        masked = jnp.where(hit, neg, masked)
        picks.append(ik)
    denom = jnp.sum(jnp.where(sel, scores, 0.0), axis=0, keepdims=True)

    r = lax.broadcasted_iota(I32, (tr, tr), 0)
    c = lax.broadcasted_iota(I32, (tr, tr), 1)
    earlier = (r < c).astype(BF16)
    sel_f = sel.astype(F32)
    before = _dot(sel_f.astype(BF16), earlier) + carry_ref[:, 0:1]
    new_carry = carry_ref[...] + jnp.sum(sel_f, axis=1, keepdims=True)
    carry_ref[...] = new_carry
    cnt_ref[...] = new_carry

    krow = lax.broadcasted_iota(I32, (TOP_K, tr), 0)
    idx_o = jnp.zeros((TOP_K, tr), I32)
    rank_o = jnp.zeros((TOP_K, tr), I32)
    w_o = jnp.zeros((TOP_K, tr), F32)
    for k, ik in enumerate(picks):
        hit = erow == ik
        wk = jnp.sum(jnp.where(hit, scores, 0.0), axis=0, keepdims=True) / denom * ROUTED_SCALE
        rk = jnp.sum(jnp.where(hit, before, 0.0), axis=0, keepdims=True).astype(I32)
        idx_o = jnp.where(krow == k, ik, idx_o)
        rank_o = jnp.where(krow == k, rk, rank_o)
        w_o = jnp.where(krow == k, wk, w_o)
    idx_ref[...] = idx_o
    rank_ref[...] = rank_o
    w_ref[...] = w_o


def _route(scores_t, router_bias_col):
    t = scores_t.shape[1]
    tr = ROUTE_TR
    col = lambda n: pl.BlockSpec((n, tr), lambda i: (0, i))
    return pl.pallas_call(
        functools.partial(_route_body, tr=tr),
        grid=(t // tr,),
        in_specs=[col(N_EXPERTS), pl.BlockSpec((N_EXPERTS, 1), lambda i: (0, 0))],
        out_specs=[col(TOP_K), col(TOP_K), col(TOP_K), pl.BlockSpec((N_EXPERTS, LANES), lambda i: (0, 0))],
        out_shape=[jax.ShapeDtypeStruct((TOP_K, t), I32), jax.ShapeDtypeStruct((TOP_K, t), F32),
                   jax.ShapeDtypeStruct((TOP_K, t), I32), jax.ShapeDtypeStruct((N_EXPERTS, LANES), F32)],
        scratch_shapes=[pltpu.VMEM((N_EXPERTS, LANES), F32)],
        compiler_params=pltpu.CompilerParams(
            dimension_semantics=("arbitrary",), vmem_limit_bytes=VMEM_LIMIT),
        name="route",
    )(scores_t, router_bias_col)


DISPATCH_TD = 128


def _row_copy(src_ref, src_row, dst_ref, dst_row, sem):
    return pltpu.make_async_copy(src_ref.at[pl.ds(src_row, 1), :], dst_ref.at[pl.ds(dst_row, 1), :], sem)


PAD_BITS = EXPERT_BM.bit_length() - 1


def _dispatch_body(ps_ref, cnt_ref, idx_ref, rank_ref, x_ref, xs_ref, zero_ref, sem, zsem, *, td, n_steps):
    step = pl.program_id(0)

    @pl.when(step == 0)
    def _():
        zero_ref[...] = jnp.zeros_like(zero_ref)

    def issue(i, carry):
        for k in range(TOP_K):
            slot = ps_ref[idx_ref[k, i]] + rank_ref[k, i]
            _row_copy(x_ref, i, xs_ref, slot, sem).start()
        return carry

    lax.fori_loop(0, td, issue, 0)

    experts_per_step = N_EXPERTS // n_steps
    pad_copies = []
    for j in range(experts_per_step):
        e = step * experts_per_step + j
        cnt = cnt_ref[e]
        base = ps_ref[e] + cnt
        n_single = (-cnt) & (SUBLANES - 1)
        for r in range(SUBLANES - 1):
            pad_copies.append((r < n_single, _row_copy(zero_ref, 0, xs_ref, base + r, zsem)))
        aligned = pl.multiple_of(base + n_single, SUBLANES)
        n_tiles = ((-cnt) & (EXPERT_BM - 1)) // SUBLANES
        for bit in range(PAD_BITS - 3):
            run = SUBLANES << bit
            pos = pl.multiple_of(aligned + SUBLANES * (n_tiles & ((1 << bit) - 1)), SUBLANES)
            pad_copies.append((
                (n_tiles & (1 << bit)) != 0,
                pltpu.make_async_copy(zero_ref.at[pl.ds(0, run), :], xs_ref.at[pl.ds(pos, run), :], zsem)))
    half = EXPERT_BM // 2
    last = N_EXPERTS - 1
    n_used = (ps_ref[last] + cnt_ref[last] + EXPERT_BM - 1) // EXPERT_BM
    n_blocks = xs_ref.shape[0] // EXPERT_BM
    for j in range(-(-(n_blocks - (td * n_steps * TOP_K) // EXPERT_BM) // n_steps)):
        blk = n_used + step + j * n_steps
        for hf in range(2):
            pos = pl.multiple_of(blk * EXPERT_BM + hf * half, half)
            pad_copies.append((
                blk < n_blocks,
                pltpu.make_async_copy(zero_ref, xs_ref.at[pl.ds(pos, half), :], zsem)))
    for cond, cp in pad_copies:
        pl.when(cond)(cp.start)
    for cond, cp in pad_copies:
        pl.when(cond)(cp.wait)

    def drain(i, carry):
        for k in range(TOP_K):
            _row_copy(x_ref, 0, xs_ref, 0, sem).wait()
        return carry

    lax.fori_loop(0, td, drain, 0)


def _dispatch(pstarts, counts, idx_t, rank_t, h1, n_slots):
    t = h1.shape[0]
    td = DISPATCH_TD
    n_steps = t // td
    assert N_EXPERTS % n_steps == 0
    smem = lambda: pl.BlockSpec((TOP_K, td), lambda i, ps, cnt: (0, i), memory_space=pltpu.SMEM)
    return pl.pallas_call(
        functools.partial(_dispatch_body, td=td, n_steps=n_steps),
        grid_spec=pltpu.PrefetchScalarGridSpec(
            num_scalar_prefetch=2,
            grid=(n_steps,),
            in_specs=[smem(), smem(),
                      pl.BlockSpec((td, D_MODEL), lambda i, ps, cnt: (i, 0))],
            out_specs=pl.BlockSpec(memory_space=pl.ANY),
            scratch_shapes=[pltpu.VMEM((EXPERT_BM // 2, D_MODEL), F32),
                            pltpu.SemaphoreType.DMA, pltpu.SemaphoreType.DMA],
        ),
        out_shape=jax.ShapeDtypeStruct((n_slots, D_MODEL), F32),
        compiler_params=pltpu.CompilerParams(
            dimension_semantics=("arbitrary",), vmem_limit_bytes=VMEM_LIMIT),
        name="dispatch",
    )(pstarts, counts, idx_t, rank_t, h1)


def _expert_body(be_ref, nu_ref, xs_ref, wg_ref, wu_ref, wd_ref, y_ref, wgb_ref, wub_ref, wdb_ref):
    i = pl.program_id(0)
    used = i < nu_ref[0]

    @pl.when(jnp.logical_and(used, jnp.logical_or(i == 0, be_ref[i] != be_ref[jnp.maximum(i - 1, 0)])))
    def _():
        wgb_ref[...] = wg_ref[0].astype(BF16)
        wub_ref[...] = wu_ref[0].astype(BF16)
        wdb_ref[...] = wd_ref[0].astype(BF16)

    @pl.when(used)
    def _():
        x = xs_ref[...].astype(BF16)
        hid = _silu(_dot(x, wgb_ref[...])) * _dot(x, wub_ref[...])
        y_ref[...] = _dot(hid.astype(BF16), wdb_ref[...])

    @pl.when(jnp.logical_not(used))
    def _():
        y_ref[...] = jnp.zeros_like(y_ref)


def _experts(block_expert, n_used, xs, w_gate_e, w_up_e, w_down_e):
    n_slots = xs.shape[0]
    bm = EXPERT_BM
    last = lambda i, nu: jnp.minimum(i, nu[0] - 1)
    return pl.pallas_call(
        _expert_body,
        grid_spec=pltpu.PrefetchScalarGridSpec(
            num_scalar_prefetch=2,
            grid=(n_slots // bm,),
            in_specs=[
                pl.BlockSpec((bm, D_MODEL), lambda i, be, nu: (last(i, nu), 0)),
                pl.BlockSpec((1, D_MODEL, EXPERT_HIDDEN), lambda i, be, nu: (be[last(i, nu)], 0, 0)),
                pl.BlockSpec((1, D_MODEL, EXPERT_HIDDEN), lambda i, be, nu: (be[last(i, nu)], 0, 0)),
                pl.BlockSpec((1, EXPERT_HIDDEN, D_MODEL), lambda i, be, nu: (be[last(i, nu)], 0, 0)),
            ],
            out_specs=pl.BlockSpec((bm, D_MODEL), lambda i, be, nu: (i, 0)),
            scratch_shapes=[pltpu.VMEM((D_MODEL, EXPERT_HIDDEN), BF16),
                            pltpu.VMEM((D_MODEL, EXPERT_HIDDEN), BF16),
                            pltpu.VMEM((EXPERT_HIDDEN, D_MODEL), BF16)],
        ),
        out_shape=jax.ShapeDtypeStruct((n_slots, D_MODEL), F32),
        compiler_params=pltpu.CompilerParams(
            dimension_semantics=("arbitrary",), vmem_limit_bytes=VMEM_LIMIT),
        name="experts",
    )(block_expert, n_used, xs, w_gate_e, w_up_e, w_down_e)


COMBINE_TC = 128


def _combine_body(ps_ref, idx_ref, rank_ref, w_ref, h1_ref, ys_ref, wgs_ref, wus_ref, wds_ref,
                  g2_ref, b2_ref, o_ref, buf, sem, *, tc):
    def issue(i, carry):
        for k in range(TOP_K):
            slot = ps_ref[idx_ref[k, i]] + rank_ref[k, i]
            _row_copy(ys_ref, slot, buf.at[k], i, sem).start()
        return carry

    lax.fori_loop(0, tc, issue, 0)

    h1 = h1_ref[...]
    hb = h1.astype(BF16)
    shared = _dot((_silu(_dot(hb, wgs_ref[...])) * _dot(hb, wus_ref[...])).astype(BF16), wds_ref[...])

    def drain(i, carry):
        for k in range(TOP_K):
            _row_copy(ys_ref, 0, buf.at[k], 0, sem).wait()
        return carry

    lax.fori_loop(0, tc, drain, 0)

    w = w_ref[...]
    routed = jnp.zeros((tc, D_MODEL), F32)
    for k in range(TOP_K):
        routed = routed + w[:, k:k + 1] * buf[k]
    o_ref[...] = _layer_norm(DN_ALPHA * h1 + routed + shared, g2_ref[...], b2_ref[...])


def _combine(pstarts, idx_t, rank_t, w_tok, h1, ys, wgs, wus, wds, g2, b2):
    t = h1.shape[0]
    tc = COMBINE_TC
    smem = lambda: pl.BlockSpec((TOP_K, tc), lambda i, ps: (0, i), memory_space=pltpu.SMEM)
    full = lambda a: pl.BlockSpec(a.shape, lambda i, ps: (0,) * a.ndim)
    return pl.pallas_call(
        functools.partial(_combine_body, tc=tc),
        grid_spec=pltpu.PrefetchScalarGridSpec(
            num_scalar_prefetch=1,
            grid=(t // tc,),
            in_specs=[smem(), smem(),
                      pl.BlockSpec((tc, TOP_K), lambda i, ps: (i, 0)),
                      pl.BlockSpec((tc, D_MODEL), lambda i, ps: (i, 0)),
                      pl.BlockSpec(memory_space=pl.ANY),
                      full(wgs), full(wus), full(wds), full(g2), full(b2)],
            out_specs=pl.BlockSpec((tc, D_MODEL), lambda i, ps: (i, 0)),
            scratch_shapes=[pltpu.VMEM((TOP_K, tc, D_MODEL), F32), pltpu.SemaphoreType.DMA],
        ),
        out_shape=jax.ShapeDtypeStruct((t, D_MODEL), F32),
        compiler_params=pltpu.CompilerParams(
            dimension_semantics=("arbitrary",), vmem_limit_bytes=VMEM_LIMIT),
        name="combine_ln2",
    )(pstarts, idx_t, rank_t, w_tok, h1, ys, wgs, wus, wds, g2, b2)


def _split_w_in(w):
    w = w.astype(BF16)
    sb = w[:, :SB_COLS]
    dq, dk, dv = (w[:, SB_COLS + j * DIL_WIDTH:SB_COLS + (j + 1) * DIL_WIDTH] for j in range(3))
    grp = lambda m, g: m[:, g * DIL_GROUP_WIDTH:(g + 1) * DIL_GROUP_WIDTH]
    dl = jnp.concatenate([grp(m, g) for g in range(N_DIL) for m in (dq, dk, dv)], axis=1)
    gates = w[:, SB_COLS + 3 * DIL_WIDTH:]
    return gates, sb, dl


def _moe(h1, scores_t, router_bias, w_gate_e, w_up_e, w_down_e, w_gate_s, w_up_s, w_down_s, g2, b2):
    t = h1.shape[0]
    idx_t, w_t, rank_t, counts = _route(scores_t, router_bias[:, None])
    counts = counts[:, 0].astype(I32)
    padded = (counts + EXPERT_BM - 1) // EXPERT_BM * EXPERT_BM
    pends = jnp.cumsum(padded)
    pstarts = (pends - padded).astype(I32)
    n_blocks = (t * TOP_K) // EXPERT_BM + N_EXPERTS
    block_start = jnp.arange(n_blocks, dtype=I32) * EXPERT_BM
    block_expert = jnp.minimum(jnp.sum((pends[None, :] <= block_start[:, None]).astype(I32), axis=1),
                               N_EXPERTS - 1).astype(I32)
    n_used = (pends[-1:] // EXPERT_BM).astype(I32)
    xs = _dispatch(pstarts, counts, idx_t, rank_t, h1, n_blocks * EXPERT_BM)
    ys = _experts(block_expert, n_used, xs, w_gate_e, w_up_e, w_down_e)
    return _combine(pstarts, idx_t, rank_t, w_t.T, h1, ys,
                    w_gate_s.astype(BF16), w_up_s.astype(BF16), w_down_s.astype(BF16),
                    g2[None], b2[None])


def kernel(x, ln_in_g, ln_in_b, rel_bias, w_in, b_gate, w_br_sb, w_br_dil, w_out, ln1_g, ln1_b,
           w_router, router_bias, w_gate_e, w_up_e, w_down_e, w_gate_s, w_up_s, w_down_s,
           ln2_g, ln2_b):
    b, s, d = x.shape
    t = b * s
    n_layers = w_in.shape[0]
    h = x.reshape(t, d)
    ln_g, ln_b = ln_in_g[None], ln_in_b[None]
    bias_tabs = [_dil_bias_table(rel_bias, g, window, dil) for g, (window, dil) in enumerate(DIL_PATTERNS)]
    for l in range(n_layers):
        assert l == 0, "single-layer block"
        w_gate, w_sb, w_dl = _split_w_in(w_in[l])
        h, gates, sb, *dl_qkv = _inproj(h, ln_g, ln_b, w_gate, w_sb, w_dl, b)
        o_sb = _sb_attention(sb.reshape(b, s, SB_COLS)).reshape(t, SB_WIDTH)
        dl = [_dilated_group(dl_qkv[g], bias_tabs[g], g) for g in range(N_DIL)]
        wr = w_router[l].T
        wr_hi = wr.astype(BF16)
        wr_lo = (wr - wr_hi.astype(F32)).astype(BF16)
        h1, scores = _merge(gates, o_sb, dl, h, b_gate[l][None], w_br_sb[l].astype(BF16),
                            w_br_dil[l].astype(BF16), w_out[l].astype(BF16),
                            ln1_g[l][None], ln1_b[l][None], wr_hi, wr_lo)
        h = _moe(h1, scores, router_bias[l], w_gate_e[l], w_up_e[l], w_down_e[l],
                 w_gate_s[l], w_up_s[l], w_down_s[l], ln2_g[l], ln2_b[l])
    return h.reshape(b, s, d)
```

```python
import functools
import math

import numpy as np
import jax
import jax.numpy as jnp
from jax import lax
from jax.experimental import pallas as pl
from jax.experimental.pallas import tpu as pltpu
from jax.experimental.pallas import tpu_sc as plsc

F32 = jnp.float32
BF16 = jnp.bfloat16
I32 = jnp.int32

D_MODEL = 1024
HEAD_DIM = 64
SB_HEADS = 8
DIL_PATTERNS = ((128, 1), (512, 4), (2048, 16))
DIL_HEADS_PER_GROUP = 4
DIL_WINDOW_KEYS = 128
REL_BUCKETS = 32
REL_MAX_DISTANCE = 2048
N_EXPERTS = 256
TOP_K = 8
N_GROUPS = 8
GROUP_SIZE = N_EXPERTS // N_GROUPS
TOPK_GROUPS = 4
EXPERT_HIDDEN = 256
ROUTED_SCALE = 2.5
LN_EPS = 1e-5
DEPTH = 1
DN_ALPHA = (2 * DEPTH) ** 0.25

SB_WIDTH = SB_HEADS * HEAD_DIM
DIL_GROUP_WIDTH = DIL_HEADS_PER_GROUP * HEAD_DIM
DIL_WIDTH = DIL_GROUP_WIDTH * len(DIL_PATTERNS)

GATE_COLS = 2 * D_MODEL
SB_COLS = 3 * SB_WIDTH
DIL_GROUP_COLS = 3 * DIL_GROUP_WIDTH
N_DIL = len(DIL_PATTERNS)

LANES = 128
SUBLANES = 8
EXPERT_BM = 256
MASK_NEG = -1e30
SB_SKIP_LOG = -104.0

VMEM_LIMIT = 48 * 1024 * 1024


def _layer_norm(x, g, b):
    mu = jnp.mean(x, axis=-1, keepdims=True)
    xc = x - mu
    var = jnp.mean(xc * xc, axis=-1, keepdims=True)
    return xc * lax.rsqrt(var + LN_EPS) * g + b


def _dot(a, b):
    return jnp.dot(a, b, preferred_element_type=F32)


def _dot_nt(a, b):
    return lax.dot_general(a, b, (((1,), (1,)), ((), ())), preferred_element_type=F32)


def _silu(x):
    return x * jax.nn.sigmoid(x)


INPROJ_TM = 512


def _inproj_body(x_ref, g_ref, b_ref, wg_ref, wsb_ref, wdl_ref, h_ref, gate_ref, sb_ref, *rest):
    dl_refs, slab_ref = rest[:N_DIL], rest[N_DIL]
    tm = x_ref.shape[0]
    h = _layer_norm(x_ref[...], g_ref[...], b_ref[...])
    h_ref[...] = h
    hb = h.astype(BF16)
    gate_ref[...] = _dot(hb, wg_ref[...]).astype(BF16)
    sb_ref[...] = _dot(hb, wsb_ref[...]).astype(BF16)
    dl = _dot(hb, wdl_ref[...])
    slabs_per_group = DIL_GROUP_COLS // LANES
    for g, (_, dilation) in enumerate(DIL_PATTERNS):
        cols = dl[:, g * DIL_GROUP_COLS:(g + 1) * DIL_GROUP_COLS]
        if dilation == 1:
            dl_refs[g][0, 0] = cols.astype(BF16)
            continue
        for s in range(slabs_per_group):
            slab_ref[s] = cols[:, s * LANES:(s + 1) * LANES]
        for r in range(dilation):
            for s in range(slabs_per_group):
                rows = slab_ref[s, pl.ds(r, tm // dilation, stride=dilation), :]
                dl_refs[g][0, r, :, s * LANES:(s + 1) * LANES] = rows.astype(BF16)


def _inproj(x2, g, b, w_gate, w_sb, w_dl, batch):
    t = x2.shape[0]
    tm = INPROJ_TM
    s = t // batch
    tiles_per_seq = s // tm
    const = lambda a: pl.BlockSpec(a.shape, lambda i: (0,) * a.ndim, pipeline_mode=pl.Buffered(1))
    row = lambda w: pl.BlockSpec((tm, w), lambda i: (i, 0))
    dl_specs, dl_shapes = [], []
    for _, dilation in DIL_PATTERNS:
        dl_specs.append(pl.BlockSpec((1, dilation, tm // dilation, DIL_GROUP_COLS),
                                     lambda i: (i // tiles_per_seq, 0, i % tiles_per_seq, 0)))
        dl_shapes.append(jax.ShapeDtypeStruct((batch, dilation, s // dilation, DIL_GROUP_COLS), BF16))
    return pl.pallas_call(
        _inproj_body,
        grid=(t // tm,),
        in_specs=[row(D_MODEL), const(g), const(b), const(w_gate), const(w_sb), const(w_dl)],
        out_specs=[row(D_MODEL), row(GATE_COLS), row(SB_COLS)] + dl_specs,
        out_shape=[jax.ShapeDtypeStruct((t, D_MODEL), F32),
                   jax.ShapeDtypeStruct((t, GATE_COLS), BF16),
                   jax.ShapeDtypeStruct((t, SB_COLS), BF16)] + dl_shapes,
        scratch_shapes=[pltpu.VMEM((DIL_GROUP_COLS // LANES, tm, LANES), F32)],
        compiler_params=pltpu.CompilerParams(
            dimension_semantics=("parallel",), vmem_limit_bytes=VMEM_LIMIT),
        name="ln_inproj",
    )(x2, g, b, w_gate, w_sb, w_dl)


SB_TQ = 256


def _sb_body(q_ref, k_ref, v_ref, o_ref, *, tq, scale):
    i = pl.program_id(2)
    q2 = q_ref[0]
    lane = lax.broadcasted_iota(I32, (tq, LANES), 1)
    zero = jnp.zeros_like(q2)
    q_a = jnp.where(lane < HEAD_DIM, q2, zero)
    q_b = jnp.where(lane >= HEAD_DIM, q2, zero)
    row = lax.broadcasted_iota(I32, (tq, tq), 0)
    col = lax.broadcasted_iota(I32, (tq, tq), 1)
    tri = col < row
    later = (row > col).astype(BF16)

    def one_head(qm, carry, acc, kb, vb, before):
        z = _dot_nt(qm, kb) * scale
        softplus = jnp.maximum(z, 0.0) + jnp.log(1.0 + jnp.exp(-jnp.abs(z)))
        log_keep = jnp.where(before, -softplus, 0.0)
        lk_hi = log_keep.astype(BF16)
        lk_lo = (log_keep - lk_hi.astype(F32)).astype(BF16)
        between = _dot(lk_hi, later) + _dot(lk_lo, later)
        total = (z - softplus) + between + carry
        w = jnp.where(before, jnp.exp(total), 0.0)
        acc = acc + _dot(w.astype(BF16), vb)
        carry = carry + jnp.sum(log_keep, axis=-1, keepdims=True)
        return carry, acc

    def cond(st):
        return jnp.logical_and(st[0] >= 0, st[1] > 0)

    def body(st):
        j, _, c_a, c_b, a_a, a_b = st
        ks = pl.multiple_of(j * tq, tq)
        kb = k_ref[0, pl.ds(ks, tq), :]
        vb = v_ref[0, pl.ds(ks, tq), :]
        before = jnp.logical_or(tri, j < i)
        c_a, a_a = one_head(q_a, c_a, a_a, kb, vb, before)
        c_b, a_b = one_head(q_b, c_b, a_b, kb, vb, before)
        go = (jnp.max(jnp.maximum(c_a, c_b)) > SB_SKIP_LOG).astype(I32)
        return j - 1, go, c_a, c_b, a_a, a_b

    c0 = jnp.zeros((tq, 1), F32)
    a0 = jnp.zeros((tq, LANES), F32)
    st = lax.while_loop(cond, body, (i, jnp.int32(1), c0, c0, a0, a0))
    o_ref[0] = jnp.where(lane < HEAD_DIM, st[4], st[5]).astype(BF16)


def _sb_attention(proj3):
    b, s, _ = proj3.shape
    tq = SB_TQ
    qb, kb, vb = 0, SB_WIDTH // LANES, 2 * SB_WIDTH // LANES
    return pl.pallas_call(
        functools.partial(_sb_body, tq=tq, scale=1.0 / math.sqrt(HEAD_DIM)),
        grid=(b, SB_WIDTH // LANES, s // tq),
        in_specs=[
            pl.BlockSpec((1, tq, LANES), lambda bi, hp, i: (bi, i, qb + hp)),
            pl.BlockSpec((1, s, LANES), lambda bi, hp, i: (bi, 0, kb + hp)),
            pl.BlockSpec((1, s, LANES), lambda bi, hp, i: (bi, 0, vb + hp)),
        ],
        out_specs=pl.BlockSpec((1, tq, LANES), lambda bi, hp, i: (bi, i, hp)),
        out_shape=jax.ShapeDtypeStruct((b, s, SB_WIDTH), BF16),
        compiler_params=pltpu.CompilerParams(
            dimension_semantics=("parallel", "parallel", "arbitrary"), vmem_limit_bytes=VMEM_LIMIT),
        name="sb_attn",
    )(proj3, proj3, proj3)


DIL_TQ = 128


def _rel_bucket_idx(dist):
    max_exact = REL_BUCKETS // 2
    d = jnp.maximum(dist, 1).astype(F32)
    large = max_exact + (jnp.log(d / max_exact) / math.log(REL_MAX_DISTANCE / max_exact)
                         * (REL_BUCKETS - max_exact)).astype(I32)
    large = jnp.minimum(large, REL_BUCKETS - 1)
    return jnp.where(dist < max_exact, dist, large)


def _dil_bias_table(rel_bias, g, window, dilation):
    del window
    a = np.arange(DIL_TQ)[:, None]
    c = np.arange(2 * DIL_TQ)[None, :]
    step = a + DIL_TQ - c
    valid = (step >= 0) & (step <= DIL_WINDOW_KEYS)
    bucket = _rel_bucket_idx(jnp.asarray(np.clip(step, 0, DIL_WINDOW_KEYS) * dilation, I32))
    heads = slice(g * DIL_HEADS_PER_GROUP, (g + 1) * DIL_HEADS_PER_GROUP)
    rb = rel_bias[:, heads].astype(F32)
    tab = jnp.zeros((DIL_HEADS_PER_GROUP,) + step.shape, F32)
    for bkt in range(REL_BUCKETS):
        tab = jnp.where(bucket[None] == bkt, rb[bkt][:, None, None], tab)
    return jnp.where(valid[None], tab, MASK_NEG)


def _dil_body(q_ref, kp_ref, kc_ref, vp_ref, vc_ref, bias_ref, o_ref, lse_ref, *, tq, scale):
    i = pl.program_id(2)
    q = q_ref[0]
    k2 = jnp.concatenate([kp_ref[0], kc_ref[0]], axis=0)
    v2 = jnp.concatenate([vp_ref[0], vc_ref[0]], axis=0)
    width = DIL_GROUP_WIDTH
    lane = lax.broadcasted_iota(I32, (tq, width), 1)
    keycol = lax.broadcasted_iota(I32, (tq, 2 * tq), 1)
    no_prev = jnp.logical_and(i == 0, keycol < tq)
    zero = jnp.zeros_like(q)
    out = jnp.zeros((tq, width), F32)
    lse_out = jnp.zeros((tq, width), F32)
    for h in range(DIL_HEADS_PER_GROUP):
        in_head = jnp.logical_and(lane >= h * HEAD_DIM, lane < (h + 1) * HEAD_DIM)
        qm = jnp.where(in_head, q, zero)
        s = _dot_nt(qm, k2) * scale + bias_ref[h]
        s = jnp.where(no_prev, MASK_NEG, s)
        m = jnp.max(s, axis=-1, keepdims=True)
        p = jnp.exp(s - m)
        l = jnp.sum(p, axis=-1, keepdims=True)
        pv = _dot(p.astype(BF16), v2)
        out = jnp.where(in_head, pv / l, out)
        lse_out = jnp.where(in_head, m + jnp.log(l), lse_out)
    o_ref[0] = out
    lse_ref[0] = lse_out


def _dilated_group(qkv, bias_tab, g):
    b, dilation, sd, _ = qkv.shape
    tq = DIL_TQ
    width = DIL_GROUP_WIDTH
    blk = (1, None, tq, width)
    o, lse = pl.pallas_call(
        functools.partial(_dil_body, tq=tq, scale=1.0 / math.sqrt(HEAD_DIM)),
        grid=(b, dilation, sd // tq),
        in_specs=[
            pl.BlockSpec(blk, lambda bi, r, i: (bi, r, i, 0)),
            pl.BlockSpec(blk, lambda bi, r, i: (bi, r, jnp.maximum(i - 1, 0), 1)),
            pl.BlockSpec(blk, lambda bi, r, i: (bi, r, i, 1)),
            pl.BlockSpec(blk, lambda bi, r, i: (bi, r, jnp.maximum(i - 1, 0), 2)),
            pl.BlockSpec(blk, lambda bi, r, i: (bi, r, i, 2)),
            pl.BlockSpec((DIL_HEADS_PER_GROUP, tq, 2 * tq), lambda bi, r, i: (0, 0, 0)),
        ],
        out_specs=[
            pl.BlockSpec((1, tq, width), lambda bi, r, i: (bi, i, r)),
            pl.BlockSpec((1, tq, width), lambda bi, r, i: (bi, i, r)),
        ],
        out_shape=[
            jax.ShapeDtypeStruct((b, sd, dilation * width), F32),
            jax.ShapeDtypeStruct((b, sd, dilation * width), F32),
        ],
        compiler_params=pltpu.CompilerParams(
            dimension_semantics=("parallel", "parallel", "arbitrary"), vmem_limit_bytes=VMEM_LIMIT),
        name=f"dil_attn_g{g}",
    )(qkv, qkv, qkv, qkv, qkv, bias_tab)
    t = b * dilation * sd
    return o.reshape(t, width), lse.reshape(t, width)


MERGE_TM = 256


def _merge_body(gp_ref, osb_ref, o0_ref, o1_ref, o2_ref, l0_ref, l1_ref, l2_ref, h_ref, bg_ref,
                wsb_ref, wdl_ref, wout_ref, g1_ref, b1_ref, wrh_ref, wrl_ref, h1_ref, sc_ref):
    l0, l1, l2 = l0_ref[...], l1_ref[...], l2_ref[...]
    m = jnp.maximum(jnp.maximum(l0, l1), l2)
    e0, e1, e2 = jnp.exp(l0 - m), jnp.exp(l1 - m), jnp.exp(l2 - m)
    o_dl = (e0 * o0_ref[...] + e1 * o1_ref[...] + e2 * o2_ref[...]) / (e0 + e1 + e2)
    br_sb = _dot(osb_ref[...], wsb_ref[...])
    br_dl = _dot(o_dl.astype(BF16), wdl_ref[...])
    gates = jax.nn.sigmoid(gp_ref[...].astype(F32) + bg_ref[...])
    merged = gates[:, :D_MODEL] * br_sb + gates[:, D_MODEL:] * br_dl
    mix = _dot(merged.astype(BF16), wout_ref[...])
    h1 = _layer_norm(DN_ALPHA * h_ref[...] + mix, g1_ref[...], b1_ref[...])
    h1_ref[...] = h1
    h_hi = h1.astype(BF16)
    h_lo = (h1 - h_hi.astype(F32)).astype(BF16)
    logits = _dot_nt(wrh_ref[...], h_hi) + _dot_nt(wrh_ref[...], h_lo) + _dot_nt(wrl_ref[...], h_hi)
    sc_ref[...] = jax.nn.sigmoid(logits)


def _merge(gates, o_sb, dl, h, b_gate, w_br_sb, w_br_dil, w_out, g1, b1, wr_hi_t, wr_lo_t):
    t = h.shape[0]
    tm = MERGE_TM
    row = lambda w: pl.BlockSpec((tm, w), lambda i: (i, 0))
    full = lambda a: pl.BlockSpec(a.shape, lambda i: (0,) * a.ndim)
    args = (gates, o_sb, dl[0][0], dl[1][0], dl[2][0], dl[0][1], dl[1][1], dl[2][1], h, b_gate,
            w_br_sb, w_br_dil, w_out, g1, b1, wr_hi_t, wr_lo_t)
    in_specs = [row(GATE_COLS), row(SB_WIDTH)] + [row(DIL_GROUP_WIDTH)] * 6 + [row(D_MODEL)] \
        + [full(a) for a in args[9:]]
    return pl.pallas_call(
        _merge_body,
        grid=(t // tm,),
        in_specs=in_specs,
        out_specs=[row(D_MODEL), pl.BlockSpec((N_EXPERTS, tm), lambda i: (0, i))],
        out_shape=[jax.ShapeDtypeStruct((t, D_MODEL), F32), jax.ShapeDtypeStruct((N_EXPERTS, t), F32)],
        compiler_params=pltpu.CompilerParams(
            dimension_semantics=("parallel",), vmem_limit_bytes=VMEM_LIMIT),
        name="merge_ln1",
    )(*args)


ROUTE_TR = 256


def _route_body(sc_ref, rb_ref, idx_ref, w_ref, rank_ref, cnt_ref, carry_ref, *, tr):
    @pl.when(pl.program_id(0) == 0)
    def _():
        carry_ref[...] = jnp.zeros_like(carry_ref)

    scores = sc_ref[...]
    biased = scores + rb_ref[...]
    erow = lax.broadcasted_iota(I32, (N_EXPERTS, tr), 0)
    neg = -jnp.inf

    def first_argmax(v, rows):
        m = jnp.max(v, axis=0, keepdims=True)
        return m, jnp.min(jnp.where(v == m, rows, N_EXPERTS), axis=0, keepdims=True)

    gslice = lambda a, g: a[g * GROUP_SIZE:(g + 1) * GROUP_SIZE, :]
    gscore = []
    grow = lax.broadcasted_iota(I32, (GROUP_SIZE, tr), 0)
    for g in range(N_GROUPS):
        vg, rg = gslice(biased, g), grow + g * GROUP_SIZE
        m1, i1 = first_argmax(vg, rg)
        m2 = jnp.max(jnp.where(rg == i1, neg, vg), axis=0, keepdims=True)
        gscore.append(m1 + m2)
    pieces = []
    for g in range(N_GROUPS):
        beaten = jnp.zeros((1, tr), I32)
        for g2 in range(N_GROUPS):
            if g2 == g:
                continue
            wins = gscore[g2] > gscore[g]
            if g2 < g:
                wins = jnp.logical_or(wins, gscore[g2] == gscore[g])
            beaten = beaten + wins.astype(I32)
        pieces.append(jnp.where(beaten < TOPK_GROUPS, gslice(biased, g), neg))
    masked = jnp.concatenate(pieces, axis=0)
    sel = jnp.zeros((N_EXPERTS, tr), jnp.bool_)
    picks = []
    for _ in range(TOP_K):
        _, ik = first_argmax(masked, erow)
        hit = erow == ik
        sel = jnp.logical_or(sel, hit)
        masked = jnp.where(hit, neg, masked)
        picks.append(ik)
    denom = jnp.sum(jnp.where(sel, scores, 0.0), axis=0, keepdims=True)

    r = lax.broadcasted_iota(I32, (tr, tr), 0)
    c = lax.broadcasted_iota(I32, (tr, tr), 1)
    earlier = (r < c).astype(BF16)
    sel_f = sel.astype(F32)
    before = _dot(sel_f.astype(BF16), earlier) + carry_ref[:, 0:1]
    new_carry = carry_ref[...] + jnp.sum(sel_f, axis=1, keepdims=True)
    carry_ref[...] = new_carry
    cnt_ref[...] = new_carry

    krow = lax.broadcasted_iota(I32, (TOP_K, tr), 0)
    idx_o = jnp.zeros((TOP_K, tr), I32)
    rank_o = jnp.zeros((TOP_K, tr), I32)
    w_o = jnp.zeros((TOP_K, tr), F32)
    for k, ik in enumerate(picks):
        hit = erow == ik
        wk = jnp.sum(jnp.where(hit, scores, 0.0), axis=0, keepdims=True) / denom * ROUTED_SCALE
        rk = jnp.sum(jnp.where(hit, before, 0.0), axis=0, keepdims=True).astype(I32)
        idx_o = jnp.where(krow == k, ik, idx_o)
        rank_o = jnp.where(krow == k, rk, rank_o)
        w_o = jnp.where(krow == k, wk, w_o)
    idx_ref[...] = idx_o
    rank_ref[...] = rank_o
    w_ref[...] = w_o


def _route(scores_t, router_bias_col):
    t = scores_t.shape[1]
    tr = ROUTE_TR
    col = lambda n: pl.BlockSpec((n, tr), lambda i: (0, i))
    return pl.pallas_call(
        functools.partial(_route_body, tr=tr),
        grid=(t // tr,),
        in_specs=[col(N_EXPERTS), pl.BlockSpec((N_EXPERTS, 1), lambda i: (0, 0))],
        out_specs=[col(TOP_K), col(TOP_K), col(TOP_K), pl.BlockSpec((N_EXPERTS, LANES), lambda i: (0, 0))],
        out_shape=[jax.ShapeDtypeStruct((TOP_K, t), I32), jax.ShapeDtypeStruct((TOP_K, t), F32),
                   jax.ShapeDtypeStruct((TOP_K, t), I32), jax.ShapeDtypeStruct((N_EXPERTS, LANES), F32)],
        scratch_shapes=[pltpu.VMEM((N_EXPERTS, LANES), F32)],
        compiler_params=pltpu.CompilerParams(
            dimension_semantics=("arbitrary",), vmem_limit_bytes=VMEM_LIMIT),
        name="route",
    )(scores_t, router_bias_col)


def _dest_body(idx_ref, rank_ref, ps_ref, dest_ref, *, tr):
    idx = idx_ref[...]
    erow = lax.broadcasted_iota(I32, (N_EXPERTS, tr), 0)
    starts = jnp.concatenate([ps_ref[...]] * (tr // LANES), axis=1)
    krow = lax.broadcasted_iota(I32, (TOP_K, tr), 0)
    out = rank_ref[...]
    for k in range(TOP_K):
        start_k = jnp.sum(jnp.where(erow == idx[k:k + 1, :], starts, 0), axis=0, keepdims=True)
        out = out + jnp.where(krow == k, start_k, 0)
    dest_ref[...] = out


def _dest(idx_t, rank_t, pstarts):
    t = idx_t.shape[1]
    tr = ROUTE_TR
    col = pl.BlockSpec((TOP_K, tr), lambda i: (0, i))
    starts = jnp.broadcast_to(pstarts[:, None], (N_EXPERTS, LANES))
    return pl.pallas_call(
        functools.partial(_dest_body, tr=tr),
        grid=(t // tr,),
        in_specs=[col, col, pl.BlockSpec((N_EXPERTS, LANES), lambda i: (0, 0))],
        out_specs=col,
        out_shape=jax.ShapeDtypeStruct((TOP_K, t), I32),
        compiler_params=pltpu.CompilerParams(
            dimension_semantics=("parallel",), vmem_limit_bytes=VMEM_LIMIT),
        name="dest",
    )(idx_t, rank_t, starts)


DISPATCH_TD = 128


def _row_copy(src_ref, src_row, dst_ref, dst_row, sem):
    return pltpu.make_async_copy(src_ref.at[pl.ds(src_row, 1), :], dst_ref.at[pl.ds(dst_row, 1), :], sem)


PAD_BITS = EXPERT_BM.bit_length() - 1


def _dispatch_body(ps_ref, cnt_ref, dest_ref, x_ref, xs_ref, zero_ref, sem, zsem, *, td, n_steps):
    step = pl.program_id(0)

    @pl.when(step == 0)
    def _():
        zero_ref[...] = jnp.zeros_like(zero_ref)

    def issue(i, carry):
        for k in range(TOP_K):
            _row_copy(x_ref, i, xs_ref, dest_ref[k, i], sem).start()
        return carry

    lax.fori_loop(0, td, issue, 0)

    experts_per_step = N_EXPERTS // n_steps
    pad_copies = []
    for j in range(experts_per_step):
        e = step * experts_per_step + j
        cnt = cnt_ref[e]
        base = ps_ref[e] + cnt
        n_single = (-cnt) & (SUBLANES - 1)
        for r in range(SUBLANES - 1):
            pad_copies.append((r < n_single, _row_copy(zero_ref, 0, xs_ref, base + r, zsem)))
        aligned = pl.multiple_of(base + n_single, SUBLANES)
        n_tiles = ((-cnt) & (EXPERT_BM - 1)) // SUBLANES
        for bit in range(PAD_BITS - 3):
            run = SUBLANES << bit
            pos = pl.multiple_of(aligned + SUBLANES * (n_tiles & ((1 << bit) - 1)), SUBLANES)
            pad_copies.append((
                (n_tiles & (1 << bit)) != 0,
                pltpu.make_async_copy(zero_ref.at[pl.ds(0, run), :], xs_ref.at[pl.ds(pos, run), :], zsem)))
    half = EXPERT_BM // 2
    last = N_EXPERTS - 1
    n_used = (ps_ref[last] + cnt_ref[last] + EXPERT_BM - 1) // EXPERT_BM
    n_blocks = xs_ref.shape[0] // EXPERT_BM
    for j in range(-(-(n_blocks - (td * n_steps * TOP_K) // EXPERT_BM) // n_steps)):
        blk = n_used + step + j * n_steps
        for hf in range(2):
            pos = pl.multiple_of(blk * EXPERT_BM + hf * half, half)
            pad_copies.append((
                blk < n_blocks,
                pltpu.make_async_copy(zero_ref, xs_ref.at[pl.ds(pos, half), :], zsem)))
    for cond, cp in pad_copies:
        pl.when(cond)(cp.start)
    for cond, cp in pad_copies:
        pl.when(cond)(cp.wait)

    def drain(i, carry):
        for k in range(TOP_K):
            _row_copy(x_ref, 0, xs_ref, 0, sem).wait()
        return carry

    lax.fori_loop(0, td, drain, 0)


def _dispatch(pstarts, counts, dest_t, h1, n_slots):
    t = h1.shape[0]
    td = DISPATCH_TD
    n_steps = t // td
    assert N_EXPERTS % n_steps == 0
    return pl.pallas_call(
        functools.partial(_dispatch_body, td=td, n_steps=n_steps),
        grid_spec=pltpu.PrefetchScalarGridSpec(
            num_scalar_prefetch=2,
            grid=(n_steps,),
            in_specs=[pl.BlockSpec((TOP_K, td), lambda i, ps, cnt: (0, i), memory_space=pltpu.SMEM),
                      pl.BlockSpec((td, D_MODEL), lambda i, ps, cnt: (i, 0))],
            out_specs=pl.BlockSpec(memory_space=pl.ANY),
            scratch_shapes=[pltpu.VMEM((EXPERT_BM // 2, D_MODEL), F32),
                            pltpu.SemaphoreType.DMA, pltpu.SemaphoreType.DMA],
        ),
        out_shape=jax.ShapeDtypeStruct((n_slots, D_MODEL), F32),
        compiler_params=pltpu.CompilerParams(
            dimension_semantics=("arbitrary",), vmem_limit_bytes=VMEM_LIMIT),
        name="dispatch",
    )(pstarts, counts, dest_t, h1)


def _expert_body(be_ref, nu_ref, xs_ref, wg_ref, wu_ref, wd_ref, y_ref, wgb_ref, wub_ref, wdb_ref):
    i = pl.program_id(0)
    used = i < nu_ref[0]

    @pl.when(jnp.logical_and(used, jnp.logical_or(i == 0, be_ref[i] != be_ref[jnp.maximum(i - 1, 0)])))
    def _():
        wgb_ref[...] = wg_ref[0].astype(BF16)
        wub_ref[...] = wu_ref[0].astype(BF16)
        wdb_ref[...] = wd_ref[0].astype(BF16)

    @pl.when(used)
    def _():
        x = xs_ref[...].astype(BF16)
        hid = _silu(_dot(x, wgb_ref[...])) * _dot(x, wub_ref[...])
        y_ref[...] = _dot(hid.astype(BF16), wdb_ref[...])

    @pl.when(jnp.logical_not(used))
    def _():
        y_ref[...] = jnp.zeros_like(y_ref)


def _experts(block_expert, n_used, xs, w_gate_e, w_up_e, w_down_e):
    n_slots = xs.shape[0]
    bm = EXPERT_BM
    last = lambda i, nu: jnp.minimum(i, nu[0] - 1)
    return pl.pallas_call(
        _expert_body,
        grid_spec=pltpu.PrefetchScalarGridSpec(
            num_scalar_prefetch=2,
            grid=(n_slots // bm,),
            in_specs=[
                pl.BlockSpec((bm, D_MODEL), lambda i, be, nu: (last(i, nu), 0)),
                pl.BlockSpec((1, D_MODEL, EXPERT_HIDDEN), lambda i, be, nu: (be[last(i, nu)], 0, 0)),
                pl.BlockSpec((1, D_MODEL, EXPERT_HIDDEN), lambda i, be, nu: (be[last(i, nu)], 0, 0)),
                pl.BlockSpec((1, EXPERT_HIDDEN, D_MODEL), lambda i, be, nu: (be[last(i, nu)], 0, 0)),
            ],
            out_specs=pl.BlockSpec((bm, D_MODEL), lambda i, be, nu: (i, 0)),
            scratch_shapes=[pltpu.VMEM((D_MODEL, EXPERT_HIDDEN), BF16),
                            pltpu.VMEM((D_MODEL, EXPERT_HIDDEN), BF16),
                            pltpu.VMEM((EXPERT_HIDDEN, D_MODEL), BF16)],
        ),
        out_shape=jax.ShapeDtypeStruct((n_slots, D_MODEL), F32),
        compiler_params=pltpu.CompilerParams(
            dimension_semantics=("arbitrary",), vmem_limit_bytes=VMEM_LIMIT),
        name="experts",
    )(block_expert, n_used, xs, w_gate_e, w_up_e, w_down_e)


SC_GATHER_WINDOW = 128
SC_GATHER_COLS = 256


def _sc_gather_rows(table, idx_row):
    m = idx_row.shape[1]
    d = table.shape[1]
    mesh = plsc.VectorSubcoreMesh(core_axis_name="core", subcore_axis_name="subcore")

    @pl.kernel(out_type=jax.ShapeDtypeStruct((m, d), table.dtype), mesh=mesh)
    def gather(x_hbm, i_hbm, o_hbm):
        for c in range(d // SC_GATHER_COLS):
            def body(i_vmem, o_vmem, c=c):
                pltpu.sync_copy(x_hbm.at[i_vmem.at[0], pl.ds(c * SC_GATHER_COLS, SC_GATHER_COLS)], o_vmem)

            pltpu.emit_pipeline(
                body,
                grid=(m // SC_GATHER_WINDOW,),
                in_specs=[pl.BlockSpec((1, SC_GATHER_WINDOW), index_map=lambda i: (0, i))],
                out_specs=[pl.BlockSpec((SC_GATHER_WINDOW, SC_GATHER_COLS), index_map=lambda i, c=c: (i, c))],
                core_axis_name=("core", "subcore"),
                dimension_semantics=(pltpu.PARALLEL,),
            )(i_hbm, o_hbm)

    return gather(table, idx_row)


COMBINE_TC = 256


def _combine_body(w_ref, h1_ref, *rest):
    rows = rest[:TOP_K]
    wgs_ref, wus_ref, wds_ref, g2_ref, b2_ref, o_ref = rest[TOP_K:]
    h1 = h1_ref[...]
    hb = h1.astype(BF16)
    acc = _dot((_silu(_dot(hb, wgs_ref[...])) * _dot(hb, wus_ref[...])).astype(BF16), wds_ref[...])
    w = w_ref[...]
    for k in range(TOP_K):
        acc = acc + w[:, k:k + 1] * rows[k][...]
    o_ref[...] = _layer_norm(DN_ALPHA * h1 + acc, g2_ref[...], b2_ref[...])


def _combine(w_tok, h1, gathered, wgs, wus, wds, g2, b2):
    t = h1.shape[0]
    tc = COMBINE_TC
    n_tiles = t // tc
    full = lambda a: pl.BlockSpec(a.shape, lambda i: (0,) * a.ndim)
    row_specs = [pl.BlockSpec((tc, D_MODEL), lambda i, k=k: (k * n_tiles + i, 0)) for k in range(TOP_K)]
    return pl.pallas_call(
        _combine_body,
        grid=(n_tiles,),
        in_specs=[pl.BlockSpec((tc, TOP_K), lambda i: (i, 0)),
                  pl.BlockSpec((tc, D_MODEL), lambda i: (i, 0))] + row_specs
        + [full(wgs), full(wus), full(wds), full(g2), full(b2)],
        out_specs=pl.BlockSpec((tc, D_MODEL), lambda i: (i, 0)),
        out_shape=jax.ShapeDtypeStruct((t, D_MODEL), F32),
        compiler_params=pltpu.CompilerParams(
            dimension_semantics=("parallel",), vmem_limit_bytes=VMEM_LIMIT),
        name="combine_ln2",
    )(w_tok, h1, *([gathered] * TOP_K), wgs, wus, wds, g2, b2)


def _split_w_in(w):
    w = w.astype(BF16)
    sb = w[:, :SB_COLS]
    dq, dk, dv = (w[:, SB_COLS + j * DIL_WIDTH:SB_COLS + (j + 1) * DIL_WIDTH] for j in range(3))
    grp = lambda m, g: m[:, g * DIL_GROUP_WIDTH:(g + 1) * DIL_GROUP_WIDTH]
    dl = jnp.concatenate([grp(m, g) for g in range(N_DIL) for m in (dq, dk, dv)], axis=1)
    gates = w[:, SB_COLS + 3 * DIL_WIDTH:]
    return gates, sb, dl


def _moe(h1, scores_t, router_bias, w_gate_e, w_up_e, w_down_e, w_gate_s, w_up_s, w_down_s, g2, b2):
    t = h1.shape[0]
    idx_t, w_t, rank_t, counts = _route(scores_t, router_bias[:, None])
    counts = counts[:, 0].astype(I32)
    padded = (counts + EXPERT_BM - 1) // EXPERT_BM * EXPERT_BM
    pends = jnp.cumsum(padded)
    pstarts = (pends - padded).astype(I32)
    n_blocks = (t * TOP_K) // EXPERT_BM + N_EXPERTS
    block_start = jnp.arange(n_blocks, dtype=I32) * EXPERT_BM
    block_expert = jnp.minimum(jnp.sum((pends[None, :] <= block_start[:, None]).astype(I32), axis=1),
                               N_EXPERTS - 1).astype(I32)
    n_used = (pends[-1:] // EXPERT_BM).astype(I32)
    dest_t = _dest(idx_t, rank_t, pstarts)
    xs = _dispatch(pstarts, counts, dest_t, h1, n_blocks * EXPERT_BM)
    ys = _experts(block_expert, n_used, xs, w_gate_e, w_up_e, w_down_e)
    gathered = _sc_gather_rows(ys, dest_t.reshape(1, TOP_K * t))
    return _combine(w_t.T, h1, gathered,
                    w_gate_s.astype(BF16), w_up_s.astype(BF16), w_down_s.astype(BF16),
                    g2[None], b2[None])


def kernel(x, ln_in_g, ln_in_b, rel_bias, w_in, b_gate, w_br_sb, w_br_dil, w_out, ln1_g, ln1_b,
           w_router, router_bias, w_gate_e, w_up_e, w_down_e, w_gate_s, w_up_s, w_down_s,
           ln2_g, ln2_b):
    b, s, d = x.shape
    t = b * s
    n_layers = w_in.shape[0]
    h = x.reshape(t, d)
    ln_g, ln_b = ln_in_g[None], ln_in_b[None]
    bias_tabs = [_dil_bias_table(rel_bias, g, window, dil) for g, (window, dil) in enumerate(DIL_PATTERNS)]
    for l in range(n_layers):
        assert l == 0, "single-layer block"
        w_gate, w_sb, w_dl = _split_w_in(w_in[l])
        h, gates, sb, *dl_qkv = _inproj(h, ln_g, ln_b, w_gate, w_sb, w_dl, b)
        o_sb = _sb_attention(sb.reshape(b, s, SB_COLS)).reshape(t, SB_WIDTH)
        dl = [_dilated_group(dl_qkv[g], bias_tabs[g], g) for g in range(N_DIL)]
        wr = w_router[l].T
        wr_hi = wr.astype(BF16)
        wr_lo = (wr - wr_hi.astype(F32)).astype(BF16)
        h1, scores = _merge(gates, o_sb, dl, h, b_gate[l][None], w_br_sb[l].astype(BF16),
                            w_br_dil[l].astype(BF16), w_out[l].astype(BF16),
                            ln1_g[l][None], ln1_b[l][None], wr_hi, wr_lo)
        h = _moe(h1, scores, router_bias[l], w_gate_e[l], w_up_e[l], w_down_e[l],
                 w_gate_s[l], w_up_s[l], w_down_s[l], ln2_g[l], ln2_b[l])
    return h.reshape(b, s, d)
```

```python
import functools
import math

import numpy as np
import jax
import jax.numpy as jnp
from jax import lax
from jax.experimental import pallas as pl
from jax.experimental.pallas import tpu as pltpu
from jax.experimental.pallas import tpu_sc as plsc

F32 = jnp.float32
BF16 = jnp.bfloat16
I32 = jnp.int32
U32 = jnp.uint32

D_MODEL = 1024
HEAD_DIM = 64
SB_HEADS = 8
DIL_PATTERNS = ((128, 1), (512, 4), (2048, 16))
DIL_HEADS_PER_GROUP = 4
DIL_WINDOW_KEYS = 128
REL_BUCKETS = 32
REL_MAX_DISTANCE = 2048
N_EXPERTS = 256
TOP_K = 8
N_GROUPS = 8
GROUP_SIZE = N_EXPERTS // N_GROUPS
TOPK_GROUPS = 4
EXPERT_HIDDEN = 256
ROUTED_SCALE = 2.5
LN_EPS = 1e-5
DEPTH = 1
DN_ALPHA = (2 * DEPTH) ** 0.25

SB_WIDTH = SB_HEADS * HEAD_DIM
DIL_GROUP_WIDTH = DIL_HEADS_PER_GROUP * HEAD_DIM
DIL_WIDTH = DIL_GROUP_WIDTH * len(DIL_PATTERNS)

GATE_COLS = 2 * D_MODEL
SB_COLS = 3 * SB_WIDTH
DIL_GROUP_COLS = 3 * DIL_GROUP_WIDTH
N_DIL = len(DIL_PATTERNS)

LANES = 128
SUBLANES = 8
EXPERT_BM = 256
MASK_NEG = -1e30
SB_SKIP_LOG = -104.0

VMEM_LIMIT = 48 * 1024 * 1024


def _layer_norm(x, g, b):
    mu = jnp.mean(x, axis=-1, keepdims=True)
    xc = x - mu
    var = jnp.mean(xc * xc, axis=-1, keepdims=True)
    return xc * lax.rsqrt(var + LN_EPS) * g + b


def _dot(a, b):
    return jnp.dot(a, b, preferred_element_type=F32)


def _dot_nt(a, b):
    return lax.dot_general(a, b, (((1,), (1,)), ((), ())), preferred_element_type=F32)


def _silu(x):
    return x * jax.nn.sigmoid(x)


INPROJ_TM = 512


def _inproj_body(x_ref, g_ref, b_ref, wg_ref, wsb_ref, wdl_ref, h_ref, gate_ref, sb_ref, *rest):
    dl_refs, slab_ref = rest[:N_DIL], rest[N_DIL]
    tm = x_ref.shape[0]
    h = _layer_norm(x_ref[...], g_ref[...], b_ref[...])
    h_ref[...] = h
    hb = h.astype(BF16)
    gate_ref[...] = _dot(hb, wg_ref[...]).astype(BF16)
    sb_ref[...] = _dot(hb, wsb_ref[...]).astype(BF16)
    dl = _dot(hb, wdl_ref[...])
    slabs_per_group = DIL_GROUP_COLS // LANES
    for g, (_, dilation) in enumerate(DIL_PATTERNS):
        cols = dl[:, g * DIL_GROUP_COLS:(g + 1) * DIL_GROUP_COLS]
        if dilation == 1:
            dl_refs[g][0, 0] = cols.astype(BF16)
            continue
        for s in range(slabs_per_group):
            slab_ref[s] = cols[:, s * LANES:(s + 1) * LANES]
        for r in range(dilation):
            for s in range(slabs_per_group):
                rows = slab_ref[s, pl.ds(r, tm // dilation, stride=dilation), :]
                dl_refs[g][0, r, :, s * LANES:(s + 1) * LANES] = rows.astype(BF16)


def _inproj(x2, g, b, w_gate, w_sb, w_dl, batch):
    t = x2.shape[0]
    tm = INPROJ_TM
    s = t // batch
    tiles_per_seq = s // tm
    const = lambda a: pl.BlockSpec(a.shape, lambda i: (0,) * a.ndim, pipeline_mode=pl.Buffered(1))
    row = lambda w: pl.BlockSpec((tm, w), lambda i: (i, 0))
    dl_specs, dl_shapes = [], []
    for _, dilation in DIL_PATTERNS:
        dl_specs.append(pl.BlockSpec((1, dilation, tm // dilation, DIL_GROUP_COLS),
                                     lambda i: (i // tiles_per_seq, 0, i % tiles_per_seq, 0)))
        dl_shapes.append(jax.ShapeDtypeStruct((batch, dilation, s // dilation, DIL_GROUP_COLS), BF16))
    return pl.pallas_call(
        _inproj_body,
        grid=(t // tm,),
        in_specs=[row(D_MODEL), const(g), const(b), const(w_gate), const(w_sb), const(w_dl)],
        out_specs=[row(D_MODEL), row(GATE_COLS), row(SB_COLS)] + dl_specs,
        out_shape=[jax.ShapeDtypeStruct((t, D_MODEL), F32),
                   jax.ShapeDtypeStruct((t, GATE_COLS), BF16),
                   jax.ShapeDtypeStruct((t, SB_COLS), BF16)] + dl_shapes,
        scratch_shapes=[pltpu.VMEM((DIL_GROUP_COLS // LANES, tm, LANES), F32)],
        compiler_params=pltpu.CompilerParams(
            dimension_semantics=("parallel",), vmem_limit_bytes=VMEM_LIMIT),
        name="ln_inproj",
    )(x2, g, b, w_gate, w_sb, w_dl)


SB_TQ = 256


def _sb_body(q_ref, k_ref, v_ref, o_ref, *, tq, scale):
    i = pl.program_id(2)
    q2 = q_ref[0]
    lane = lax.broadcasted_iota(I32, (tq, LANES), 1)
    zero = jnp.zeros_like(q2)
    q_a = jnp.where(lane < HEAD_DIM, q2, zero)
    q_b = jnp.where(lane >= HEAD_DIM, q2, zero)
    row = lax.broadcasted_iota(I32, (tq, tq), 0)
    col = lax.broadcasted_iota(I32, (tq, tq), 1)
    tri = col < row
    later = (row > col).astype(BF16)

    def one_head(qm, carry, acc, kb, vb, before):
        z = _dot_nt(qm, kb) * scale
        softplus = jnp.maximum(z, 0.0) + jnp.log(1.0 + jnp.exp(-jnp.abs(z)))
        log_keep = jnp.where(before, -softplus, 0.0)
        lk_hi = log_keep.astype(BF16)
        lk_lo = (log_keep - lk_hi.astype(F32)).astype(BF16)
        between = _dot(lk_hi, later) + _dot(lk_lo, later)
        total = (z - softplus) + between + carry
        w = jnp.where(before, jnp.exp(total), 0.0)
        acc = acc + _dot(w.astype(BF16), vb)
        carry = carry + jnp.sum(log_keep, axis=-1, keepdims=True)
        return carry, acc

    def cond(st):
        return jnp.logical_and(st[0] >= 0, st[1] > 0)

    def body(st):
        j, _, c_a, c_b, a_a, a_b = st
        ks = pl.multiple_of(j * tq, tq)
        kb = k_ref[0, pl.ds(ks, tq), :]
        vb = v_ref[0, pl.ds(ks, tq), :]
        before = jnp.logical_or(tri, j < i)
        c_a, a_a = one_head(q_a, c_a, a_a, kb, vb, before)
        c_b, a_b = one_head(q_b, c_b, a_b, kb, vb, before)
        go = (jnp.max(jnp.maximum(c_a, c_b)) > SB_SKIP_LOG).astype(I32)
        return j - 1, go, c_a, c_b, a_a, a_b

    c0 = jnp.zeros((tq, 1), F32)
    a0 = jnp.zeros((tq, LANES), F32)
    st = lax.while_loop(cond, body, (i, jnp.int32(1), c0, c0, a0, a0))
    o_ref[0] = jnp.where(lane < HEAD_DIM, st[4], st[5]).astype(BF16)


def _sb_attention(proj3):
    b, s, _ = proj3.shape
    tq = SB_TQ
    qb, kb, vb = 0, SB_WIDTH // LANES, 2 * SB_WIDTH // LANES
    return pl.pallas_call(
        functools.partial(_sb_body, tq=tq, scale=1.0 / math.sqrt(HEAD_DIM)),
        grid=(b, SB_WIDTH // LANES, s // tq),
        in_specs=[
            pl.BlockSpec((1, tq, LANES), lambda bi, hp, i: (bi, i, qb + hp)),
            pl.BlockSpec((1, s, LANES), lambda bi, hp, i: (bi, 0, kb + hp)),
            pl.BlockSpec((1, s, LANES), lambda bi, hp, i: (bi, 0, vb + hp)),
        ],
        out_specs=pl.BlockSpec((1, tq, LANES), lambda bi, hp, i: (bi, i, hp)),
        out_shape=jax.ShapeDtypeStruct((b, s, SB_WIDTH), BF16),
        compiler_params=pltpu.CompilerParams(
            dimension_semantics=("parallel", "parallel", "arbitrary"), vmem_limit_bytes=VMEM_LIMIT),
        name="sb_attn",
    )(proj3, proj3, proj3)


DIL_TQ = 128


def _rel_bucket_idx(dist):
    max_exact = REL_BUCKETS // 2
    d = jnp.maximum(dist, 1).astype(F32)
    large = max_exact + (jnp.log(d / max_exact) / math.log(REL_MAX_DISTANCE / max_exact)
                         * (REL_BUCKETS - max_exact)).astype(I32)
    large = jnp.minimum(large, REL_BUCKETS - 1)
    return jnp.where(dist < max_exact, dist, large)


def _dil_bias_table(rel_bias, g, window, dilation):
    del window
    a = np.arange(DIL_TQ)[:, None]
    c = np.arange(2 * DIL_TQ)[None, :]
    step = a + DIL_TQ - c
    valid = (step >= 0) & (step <= DIL_WINDOW_KEYS)
    bucket = _rel_bucket_idx(jnp.asarray(np.clip(step, 0, DIL_WINDOW_KEYS) * dilation, I32))
    heads = slice(g * DIL_HEADS_PER_GROUP, (g + 1) * DIL_HEADS_PER_GROUP)
    rb = rel_bias[:, heads].astype(F32)
    tab = jnp.zeros((DIL_HEADS_PER_GROUP,) + step.shape, F32)
    for bkt in range(REL_BUCKETS):
        tab = jnp.where(bucket[None] == bkt, rb[bkt][:, None, None], tab)
    return jnp.where(valid[None], tab, MASK_NEG)


def _dil_body(q_ref, kp_ref, kc_ref, vp_ref, vc_ref, bias_ref, o_ref, lse_ref, *, tq, scale):
    i = pl.program_id(2)
    q = q_ref[0]
    k2 = jnp.concatenate([kp_ref[0], kc_ref[0]], axis=0)
    v2 = jnp.concatenate([vp_ref[0], vc_ref[0]], axis=0)
    width = DIL_GROUP_WIDTH
    lane = lax.broadcasted_iota(I32, (tq, width), 1)
    keycol = lax.broadcasted_iota(I32, (tq, 2 * tq), 1)
    no_prev = jnp.logical_and(i == 0, keycol < tq)
    zero = jnp.zeros_like(q)
    out = jnp.zeros((tq, width), F32)
    lse_out = jnp.zeros((tq, width), F32)
    for h in range(DIL_HEADS_PER_GROUP):
        in_head = jnp.logical_and(lane >= h * HEAD_DIM, lane < (h + 1) * HEAD_DIM)
        qm = jnp.where(in_head, q, zero)
        s = _dot_nt(qm, k2) * scale + bias_ref[h]
        s = jnp.where(no_prev, MASK_NEG, s)
        m = jnp.max(s, axis=-1, keepdims=True)
        p = jnp.exp(s - m)
        l = jnp.sum(p, axis=-1, keepdims=True)
        pv = _dot(p.astype(BF16), v2)
        out = jnp.where(in_head, pv / l, out)
        lse_out = jnp.where(in_head, m + jnp.log(l), lse_out)
    o_ref[0] = out
    lse_ref[0] = lse_out


def _dilated_group(qkv, bias_tab, g):
    b, dilation, sd, _ = qkv.shape
    tq = DIL_TQ
    width = DIL_GROUP_WIDTH
    blk = (1, None, tq, width)
    o, lse = pl.pallas_call(
        functools.partial(_dil_body, tq=tq, scale=1.0 / math.sqrt(HEAD_DIM)),
        grid=(b, dilation, sd // tq),
        in_specs=[
            pl.BlockSpec(blk, lambda bi, r, i: (bi, r, i, 0)),
            pl.BlockSpec(blk, lambda bi, r, i: (bi, r, jnp.maximum(i - 1, 0), 1)),
            pl.BlockSpec(blk, lambda bi, r, i: (bi, r, i, 1)),
            pl.BlockSpec(blk, lambda bi, r, i: (bi, r, jnp.maximum(i - 1, 0), 2)),
            pl.BlockSpec(blk, lambda bi, r, i: (bi, r, i, 2)),
            pl.BlockSpec((DIL_HEADS_PER_GROUP, tq, 2 * tq), lambda bi, r, i: (0, 0, 0)),
        ],
        out_specs=[
            pl.BlockSpec((1, tq, width), lambda bi, r, i: (bi, i, r)),
            pl.BlockSpec((1, tq, width), lambda bi, r, i: (bi, i, r)),
        ],
        out_shape=[
            jax.ShapeDtypeStruct((b, sd, dilation * width), F32),
            jax.ShapeDtypeStruct((b, sd, dilation * width), F32),
        ],
        compiler_params=pltpu.CompilerParams(
            dimension_semantics=("parallel", "parallel", "arbitrary"), vmem_limit_bytes=VMEM_LIMIT),
        name=f"dil_attn_g{g}",
    )(qkv, qkv, qkv, qkv, qkv, bias_tab)
    t = b * dilation * sd
    return o.reshape(t, width), lse.reshape(t, width)


MERGE_TM = 256


def _merge_body(gp_ref, osb_ref, o0_ref, o1_ref, o2_ref, l0_ref, l1_ref, l2_ref, h_ref, bg_ref,
                wsb_ref, wdl_ref, wout_ref, g1_ref, b1_ref, wrh_ref, wrl_ref, h1_ref, h1p_ref, sc_ref):
    l0, l1, l2 = l0_ref[...], l1_ref[...], l2_ref[...]
    m = jnp.maximum(jnp.maximum(l0, l1), l2)
    e0, e1, e2 = jnp.exp(l0 - m), jnp.exp(l1 - m), jnp.exp(l2 - m)
    o_dl = (e0 * o0_ref[...] + e1 * o1_ref[...] + e2 * o2_ref[...]) / (e0 + e1 + e2)
    br_sb = _dot(osb_ref[...], wsb_ref[...])
    br_dl = _dot(o_dl.astype(BF16), wdl_ref[...])
    gates = jax.nn.sigmoid(gp_ref[...].astype(F32) + bg_ref[...])
    merged = gates[:, :D_MODEL] * br_sb + gates[:, D_MODEL:] * br_dl
    mix = _dot(merged.astype(BF16), wout_ref[...])
    h1 = _layer_norm(DN_ALPHA * h_ref[...] + mix, g1_ref[...], b1_ref[...])
    h1_ref[...] = h1
    h1p_ref[...] = _pack_rows(h1)
    h_hi = h1.astype(BF16)
    h_lo = (h1 - h_hi.astype(F32)).astype(BF16)
    logits = _dot_nt(wrh_ref[...], h_hi) + _dot_nt(wrh_ref[...], h_lo) + _dot_nt(wrl_ref[...], h_hi)
    sc_ref[...] = jax.nn.sigmoid(logits)


def _merge(gates, o_sb, dl, h, b_gate, w_br_sb, w_br_dil, w_out, g1, b1, wr_hi_t, wr_lo_t):
    t = h.shape[0]
    tm = MERGE_TM
    row = lambda w: pl.BlockSpec((tm, w), lambda i: (i, 0))
    full = lambda a: pl.BlockSpec(a.shape, lambda i: (0,) * a.ndim)
    args = (gates, o_sb, dl[0][0], dl[1][0], dl[2][0], dl[0][1], dl[1][1], dl[2][1], h, b_gate,
            w_br_sb, w_br_dil, w_out, g1, b1, wr_hi_t, wr_lo_t)
    in_specs = [row(GATE_COLS), row(SB_WIDTH)] + [row(DIL_GROUP_WIDTH)] * 6 + [row(D_MODEL)] \
        + [full(a) for a in args[9:]]
    return pl.pallas_call(
        _merge_body,
        grid=(t // tm,),
        in_specs=in_specs,
        out_specs=[row(D_MODEL), row(D_MODEL // 2), pl.BlockSpec((N_EXPERTS, tm), lambda i: (0, i))],
        out_shape=[jax.ShapeDtypeStruct((t, D_MODEL), F32), jax.ShapeDtypeStruct((t, D_MODEL // 2), U32),
                   jax.ShapeDtypeStruct((N_EXPERTS, t), F32)],
        compiler_params=pltpu.CompilerParams(
            dimension_semantics=("parallel",), vmem_limit_bytes=VMEM_LIMIT),
        name="merge_ln1",
    )(*args)


ROUTE_TR = 256


def _route_body(sc_ref, rb_ref, idx_ref, w_ref, rank_ref, cnt_ref, carry_ref, *, tr):
    @pl.when(pl.program_id(0) == 0)
    def _():
        carry_ref[...] = jnp.zeros_like(carry_ref)

    scores = sc_ref[...]
    biased = scores + rb_ref[...]
    erow = lax.broadcasted_iota(I32, (N_EXPERTS, tr), 0)
    neg = -jnp.inf

    def first_argmax(v, rows):
        m = jnp.max(v, axis=0, keepdims=True)
        return m, jnp.min(jnp.where(v == m, rows, N_EXPERTS), axis=0, keepdims=True)

    gslice = lambda a, g: a[g * GROUP_SIZE:(g + 1) * GROUP_SIZE, :]
    gscore = []
    grow = lax.broadcasted_iota(I32, (GROUP_SIZE, tr), 0)
    for g in range(N_GROUPS):
        vg, rg = gslice(biased, g), grow + g * GROUP_SIZE
        m1, i1 = first_argmax(vg, rg)
        m2 = jnp.max(jnp.where(rg == i1, neg, vg), axis=0, keepdims=True)
        gscore.append(m1 + m2)
    pieces = []
    for g in range(N_GROUPS):
        beaten = jnp.zeros((1, tr), I32)
        for g2 in range(N_GROUPS):
            if g2 == g:
                continue
            wins = gscore[g2] > gscore[g]
            if g2 < g:
                wins = jnp.logical_or(wins, gscore[g2] == gscore[g])
            beaten = beaten + wins.astype(I32)
        pieces.append(jnp.where(beaten < TOPK_GROUPS, gslice(biased, g), neg))
    masked = jnp.concatenate(pieces, axis=0)
    sel = jnp.zeros((N_EXPERTS, tr), jnp.bool_)
    picks = []
    for _ in range(TOP_K):
        _, ik = first_argmax(masked, erow)
        hit = erow == ik
        sel = jnp.logical_or(sel, hit)
        masked = jnp.where(hit, neg, masked)
        picks.append(ik)
    denom = jnp.sum(jnp.where(sel, scores, 0.0), axis=0, keepdims=True)

    r = lax.broadcasted_iota(I32, (tr, tr), 0)
    c = lax.broadcasted_iota(I32, (tr, tr), 1)
    earlier = (r < c).astype(BF16)
    sel_f = sel.astype(F32)
    before = _dot(sel_f.astype(BF16), earlier) + carry_ref[:, 0:1]
    new_carry = carry_ref[...] + jnp.sum(sel_f, axis=1, keepdims=True)
    carry_ref[...] = new_carry
    cnt_ref[...] = new_carry

    krow = lax.broadcasted_iota(I32, (TOP_K, tr), 0)
    idx_o = jnp.zeros((TOP_K, tr), I32)
    rank_o = jnp.zeros((TOP_K, tr), I32)
    w_o = jnp.zeros((TOP_K, tr), F32)
    for k, ik in enumerate(picks):
        hit = erow == ik
        wk = jnp.sum(jnp.where(hit, scores, 0.0), axis=0, keepdims=True) / denom * ROUTED_SCALE
        rk = jnp.sum(jnp.where(hit, before, 0.0), axis=0, keepdims=True).astype(I32)
        idx_o = jnp.where(krow == k, ik, idx_o)
        rank_o = jnp.where(krow == k, rk, rank_o)
        w_o = jnp.where(krow == k, wk, w_o)
    idx_ref[...] = idx_o
    rank_ref[...] = rank_o
    w_ref[...] = w_o


def _route(scores_t, router_bias_col):
    t = scores_t.shape[1]
    tr = ROUTE_TR
    col = lambda n: pl.BlockSpec((n, tr), lambda i: (0, i))
    return pl.pallas_call(
        functools.partial(_route_body, tr=tr),
        grid=(t // tr,),
        in_specs=[col(N_EXPERTS), pl.BlockSpec((N_EXPERTS, 1), lambda i: (0, 0))],
        out_specs=[col(TOP_K), col(TOP_K), col(TOP_K), pl.BlockSpec((N_EXPERTS, LANES), lambda i: (0, 0))],
        out_shape=[jax.ShapeDtypeStruct((TOP_K, t), I32), jax.ShapeDtypeStruct((TOP_K, t), F32),
                   jax.ShapeDtypeStruct((TOP_K, t), I32), jax.ShapeDtypeStruct((N_EXPERTS, LANES), F32)],
        scratch_shapes=[pltpu.VMEM((N_EXPERTS, LANES), F32)],
        compiler_params=pltpu.CompilerParams(
            dimension_semantics=("arbitrary",), vmem_limit_bytes=VMEM_LIMIT),
        name="route",
    )(scores_t, router_bias_col)


def _dest_body(idx_ref, rank_ref, ps_ref, dest_ref, *, tr):
    idx = idx_ref[...]
    erow = lax.broadcasted_iota(I32, (N_EXPERTS, tr), 0)
    starts = jnp.concatenate([ps_ref[...]] * (tr // LANES), axis=1)
    krow = lax.broadcasted_iota(I32, (TOP_K, tr), 0)
    out = rank_ref[...]
    for k in range(TOP_K):
        start_k = jnp.sum(jnp.where(erow == idx[k:k + 1, :], starts, 0), axis=0, keepdims=True)
        out = out + jnp.where(krow == k, start_k, 0)
    dest_ref[...] = out


def _dest(idx_t, rank_t, pstarts):
    t = idx_t.shape[1]
    tr = ROUTE_TR
    col = pl.BlockSpec((TOP_K, tr), lambda i: (0, i))
    starts = jnp.broadcast_to(pstarts[:, None], (N_EXPERTS, LANES))
    return pl.pallas_call(
        functools.partial(_dest_body, tr=tr),
        grid=(t // tr,),
        in_specs=[col, col, pl.BlockSpec((N_EXPERTS, LANES), lambda i: (0, 0))],
        out_specs=col,
        out_shape=jax.ShapeDtypeStruct((TOP_K, t), I32),
        compiler_params=pltpu.CompilerParams(
            dimension_semantics=("parallel",), vmem_limit_bytes=VMEM_LIMIT),
        name="dest",
    )(idx_t, rank_t, starts)


SC_WINDOW = 128
SC_COLS = 256
HALF = D_MODEL // 2


def _pack_rows(x):
    bits = lambda v: lax.bitcast_convert_type(v.astype(BF16).astype(F32), U32)
    return bits(x[:, HALF:]) | (bits(x[:, :HALF]) >> 16)


def _unpack_rows(w):
    lo = lax.bitcast_convert_type(w << 16, F32)
    hi = lax.bitcast_convert_type(w & jnp.uint32(0xFFFF0000), F32)
    return lo, hi


def _sc_mesh():
    return plsc.VectorSubcoreMesh(core_axis_name="core", subcore_axis_name="subcore")


def _sc_scatter_rows(x, dest_t, n_slots):
    t, width = x.shape

    @pl.kernel(out_type=jax.ShapeDtypeStruct((n_slots, width), x.dtype), mesh=_sc_mesh())
    def scatter(x_hbm, i_hbm, o_hbm):
        for k in range(TOP_K):
            for c in range(width // SC_COLS):
                def body(x_vmem, i_vmem, c=c):
                    pltpu.sync_copy(x_vmem, o_hbm.at[i_vmem.at[0], pl.ds(c * SC_COLS, SC_COLS)])

                pltpu.emit_pipeline(
                    body,
                    grid=(t // SC_WINDOW,),
                    in_specs=[pl.BlockSpec((SC_WINDOW, SC_COLS), index_map=lambda i, c=c: (i, c)),
                              pl.BlockSpec((1, SC_WINDOW), index_map=lambda i, k=k: (k, i))],
                    out_specs=[],
                    core_axis_name=("core", "subcore"),
                    dimension_semantics=(pltpu.PARALLEL,),
                )(x_hbm, i_hbm)

    return scatter(x, dest_t)


def _expert_body(be_ref, bv_ref, nu_ref, xs_ref, wg_ref, wu_ref, wd_ref, y_ref, wgb_ref, wub_ref, wdb_ref):
    i = pl.program_id(0)
    used = i < nu_ref[0]

    @pl.when(jnp.logical_and(used, jnp.logical_or(i == 0, be_ref[i] != be_ref[jnp.maximum(i - 1, 0)])))
    def _():
        wgb_ref[...] = wg_ref[0].astype(BF16)
        wub_ref[...] = wu_ref[0].astype(BF16)
        wdb_ref[...] = wd_ref[0].astype(BF16)

    @pl.when(used)
    def _():
        words = xs_ref[...]
        row = lax.broadcasted_iota(I32, words.shape, 0)
        lo, hi = _unpack_rows(jnp.where(row < bv_ref[i], words, jnp.uint32(0)))
        lo, hi = lo.astype(BF16), hi.astype(BF16)
        gate = _dot(lo, wgb_ref[:HALF, :]) + _dot(hi, wgb_ref[HALF:, :])
        up = _dot(lo, wub_ref[:HALF, :]) + _dot(hi, wub_ref[HALF:, :])
        y_ref[...] = _pack_rows(_dot((_silu(gate) * up).astype(BF16), wdb_ref[...]))

    @pl.when(jnp.logical_not(used))
    def _():
        y_ref[...] = jnp.zeros_like(y_ref)


def _experts(block_expert, block_valid, n_used, xs, w_gate_e, w_up_e, w_down_e):
    n_slots = xs.shape[0]
    bm = EXPERT_BM
    last = lambda i, nu: jnp.minimum(i, nu[0] - 1)
    return pl.pallas_call(
        _expert_body,
        grid_spec=pltpu.PrefetchScalarGridSpec(
            num_scalar_prefetch=3,
            grid=(n_slots // bm,),
            in_specs=[
                pl.BlockSpec((bm, HALF), lambda i, be, bv, nu: (last(i, nu), 0)),
                pl.BlockSpec((1, D_MODEL, EXPERT_HIDDEN), lambda i, be, bv, nu: (be[last(i, nu)], 0, 0)),
                pl.BlockSpec((1, D_MODEL, EXPERT_HIDDEN), lambda i, be, bv, nu: (be[last(i, nu)], 0, 0)),
                pl.BlockSpec((1, EXPERT_HIDDEN, D_MODEL), lambda i, be, bv, nu: (be[last(i, nu)], 0, 0)),
            ],
            out_specs=pl.BlockSpec((bm, HALF), lambda i, be, bv, nu: (i, 0)),
            scratch_shapes=[pltpu.VMEM((D_MODEL, EXPERT_HIDDEN), BF16),
                            pltpu.VMEM((D_MODEL, EXPERT_HIDDEN), BF16),
                            pltpu.VMEM((EXPERT_HIDDEN, D_MODEL), BF16)],
        ),
        out_shape=jax.ShapeDtypeStruct((n_slots, HALF), U32),
        compiler_params=pltpu.CompilerParams(
            dimension_semantics=("arbitrary",), vmem_limit_bytes=VMEM_LIMIT),
        name="experts",
    )(block_expert, block_valid, n_used, xs, w_gate_e, w_up_e, w_down_e)


def _sc_gather_rows(table, idx_row):
    m = idx_row.shape[1]
    width = table.shape[1]

    @pl.kernel(out_type=jax.ShapeDtypeStruct((m, width), table.dtype), mesh=_sc_mesh())
    def gather(x_hbm, i_hbm, o_hbm):
        for c in range(width // SC_COLS):
            def body(i_vmem, o_vmem, c=c):
                pltpu.sync_copy(x_hbm.at[i_vmem.at[0], pl.ds(c * SC_COLS, SC_COLS)], o_vmem)

            pltpu.emit_pipeline(
                body,
                grid=(m // SC_WINDOW,),
                in_specs=[pl.BlockSpec((1, SC_WINDOW), index_map=lambda i: (0, i))],
                out_specs=[pl.BlockSpec((SC_WINDOW, SC_COLS), index_map=lambda i, c=c: (i, c))],
                core_axis_name=("core", "subcore"),
                dimension_semantics=(pltpu.PARALLEL,),
            )(i_hbm, o_hbm)

    return gather(table, idx_row)


COMBINE_TC = 256


def _combine_body(w_ref, h1_ref, *rest):
    rows = rest[:TOP_K]
    wgs_ref, wus_ref, wds_ref, g2_ref, b2_ref, o_ref = rest[TOP_K:]
    h1 = h1_ref[...]
    hb = h1.astype(BF16)
    shared = _dot((_silu(_dot(hb, wgs_ref[...])) * _dot(hb, wus_ref[...])).astype(BF16), wds_ref[...])
    w = w_ref[...]
    acc_lo = jnp.zeros((h1.shape[0], HALF), F32)
    acc_hi = jnp.zeros((h1.shape[0], HALF), F32)
    for k in range(TOP_K):
        lo, hi = _unpack_rows(rows[k][...])
        acc_lo = acc_lo + w[:, k:k + 1] * lo
        acc_hi = acc_hi + w[:, k:k + 1] * hi
    routed = jnp.concatenate([acc_lo, acc_hi], axis=1)
    o_ref[...] = _layer_norm(DN_ALPHA * h1 + routed + shared, g2_ref[...], b2_ref[...])


def _combine(w_tok, h1, gathered, wgs, wus, wds, g2, b2):
    t = h1.shape[0]
    tc = COMBINE_TC
    n_tiles = t // tc
    full = lambda a: pl.BlockSpec(a.shape, lambda i: (0,) * a.ndim)
    row_specs = [pl.BlockSpec((tc, HALF), lambda i, k=k: (k * n_tiles + i, 0)) for k in range(TOP_K)]
    return pl.pallas_call(
        _combine_body,
        grid=(n_tiles,),
        in_specs=[pl.BlockSpec((tc, TOP_K), lambda i: (i, 0)),
                  pl.BlockSpec((tc, D_MODEL), lambda i: (i, 0))] + row_specs
        + [full(wgs), full(wus), full(wds), full(g2), full(b2)],
        out_specs=pl.BlockSpec((tc, D_MODEL), lambda i: (i, 0)),
        out_shape=jax.ShapeDtypeStruct((t, D_MODEL), F32),
        compiler_params=pltpu.CompilerParams(
            dimension_semantics=("parallel",), vmem_limit_bytes=VMEM_LIMIT),
        name="combine_ln2",
    )(w_tok, h1, *([gathered] * TOP_K), wgs, wus, wds, g2, b2)


def _split_w_in(w):
    w = w.astype(BF16)
    sb = w[:, :SB_COLS]
    dq, dk, dv = (w[:, SB_COLS + j * DIL_WIDTH:SB_COLS + (j + 1) * DIL_WIDTH] for j in range(3))
    grp = lambda m, g: m[:, g * DIL_GROUP_WIDTH:(g + 1) * DIL_GROUP_WIDTH]
    dl = jnp.concatenate([grp(m, g) for g in range(N_DIL) for m in (dq, dk, dv)], axis=1)
    gates = w[:, SB_COLS + 3 * DIL_WIDTH:]
    return gates, sb, dl


def _moe(h1, h1_packed, scores_t, router_bias, w_gate_e, w_up_e, w_down_e, w_gate_s, w_up_s, w_down_s, g2, b2):
    t = h1.shape[0]
    idx_t, w_t, rank_t, counts = _route(scores_t, router_bias[:, None])
    counts = counts[:, 0].astype(I32)
    padded = (counts + EXPERT_BM - 1) // EXPERT_BM * EXPERT_BM
    pends = jnp.cumsum(padded)
    pstarts = (pends - padded).astype(I32)
    n_blocks = (t * TOP_K) // EXPERT_BM + N_EXPERTS
    block_start = jnp.arange(n_blocks, dtype=I32) * EXPERT_BM
    block_expert = jnp.minimum(jnp.sum((pends[None, :] <= block_start[:, None]).astype(I32), axis=1),
                               N_EXPERTS - 1).astype(I32)
    n_used = (pends[-1:] // EXPERT_BM).astype(I32)
    block_valid = jnp.clip(counts[block_expert] + pstarts[block_expert] - block_start, 0, EXPERT_BM).astype(I32)
    dest_t = _dest(idx_t, rank_t, pstarts)
    xs = _sc_scatter_rows(h1_packed, dest_t, n_blocks * EXPERT_BM)
    ys = _experts(block_expert, block_valid, n_used, xs, w_gate_e, w_up_e, w_down_e)
    gathered = _sc_gather_rows(ys, dest_t.reshape(1, TOP_K * t))
    return _combine(w_t.T, h1, gathered,
                    w_gate_s.astype(BF16), w_up_s.astype(BF16), w_down_s.astype(BF16),
                    g2[None], b2[None])


def kernel(x, ln_in_g, ln_in_b, rel_bias, w_in, b_gate, w_br_sb, w_br_dil, w_out, ln1_g, ln1_b,
           w_router, router_bias, w_gate_e, w_up_e, w_down_e, w_gate_s, w_up_s, w_down_s,
           ln2_g, ln2_b):
    b, s, d = x.shape
    t = b * s
    n_layers = w_in.shape[0]
    h = x.reshape(t, d)
    ln_g, ln_b = ln_in_g[None], ln_in_b[None]
    bias_tabs = [_dil_bias_table(rel_bias, g, window, dil) for g, (window, dil) in enumerate(DIL_PATTERNS)]
    for l in range(n_layers):
        assert l == 0, "single-layer block"
        w_gate, w_sb, w_dl = _split_w_in(w_in[l])
        h, gates, sb, *dl_qkv = _inproj(h, ln_g, ln_b, w_gate, w_sb, w_dl, b)
        o_sb = _sb_attention(sb.reshape(b, s, SB_COLS)).reshape(t, SB_WIDTH)
        dl = [_dilated_group(dl_qkv[g], bias_tabs[g], g) for g in range(N_DIL)]
        wr = w_router[l].T
        wr_hi = wr.astype(BF16)
        wr_lo = (wr - wr_hi.astype(F32)).astype(BF16)
        h1, h1_packed, scores = _merge(gates, o_sb, dl, h, b_gate[l][None], w_br_sb[l].astype(BF16),
                                       w_br_dil[l].astype(BF16), w_out[l].astype(BF16),
                                       ln1_g[l][None], ln1_b[l][None], wr_hi, wr_lo)
        h = _moe(h1, h1_packed, scores, router_bias[l], w_gate_e[l], w_up_e[l], w_down_e[l],
                 w_gate_s[l], w_up_s[l], w_down_s[l], ln2_g[l], ln2_b[l])
    return h.reshape(b, s, d)
```

```python
import functools
import math

import numpy as np
import jax
import jax.numpy as jnp
from jax import lax
from jax.experimental import pallas as pl
from jax.experimental.pallas import tpu as pltpu
from jax.experimental.pallas import tpu_sc as plsc

F32 = jnp.float32
BF16 = jnp.bfloat16
I32 = jnp.int32
U32 = jnp.uint32

D_MODEL = 1024
HEAD_DIM = 64
SB_HEADS = 8
DIL_PATTERNS = ((128, 1), (512, 4), (2048, 16))
DIL_HEADS_PER_GROUP = 4
DIL_WINDOW_KEYS = 128
REL_BUCKETS = 32
REL_MAX_DISTANCE = 2048
N_EXPERTS = 256
TOP_K = 8
N_GROUPS = 8
GROUP_SIZE = N_EXPERTS // N_GROUPS
TOPK_GROUPS = 4
EXPERT_HIDDEN = 256
ROUTED_SCALE = 2.5
LN_EPS = 1e-5
DEPTH = 1
DN_ALPHA = (2 * DEPTH) ** 0.25

SB_WIDTH = SB_HEADS * HEAD_DIM
DIL_GROUP_WIDTH = DIL_HEADS_PER_GROUP * HEAD_DIM
DIL_WIDTH = DIL_GROUP_WIDTH * len(DIL_PATTERNS)

GATE_COLS = 2 * D_MODEL
SB_COLS = 3 * SB_WIDTH
DIL_GROUP_COLS = 3 * DIL_GROUP_WIDTH
N_DIL = len(DIL_PATTERNS)

LANES = 128
SUBLANES = 8
EXPERT_BM = 256
MASK_NEG = -1e30
SB_SKIP_LOG = -104.0

VMEM_LIMIT = 48 * 1024 * 1024


def _layer_norm(x, g, b):
    mu = jnp.mean(x, axis=-1, keepdims=True)
    xc = x - mu
    var = jnp.mean(xc * xc, axis=-1, keepdims=True)
    return xc * lax.rsqrt(var + LN_EPS) * g + b


def _dot(a, b):
    return jnp.dot(a, b, preferred_element_type=F32)


def _dot_nt(a, b):
    return lax.dot_general(a, b, (((1,), (1,)), ((), ())), preferred_element_type=F32)


def _silu(x):
    return x * jax.nn.sigmoid(x)


INPROJ_TM = 512


def _inproj_body(x_ref, g_ref, b_ref, wg_ref, wsb_ref, wdl_ref, h_ref, gate_ref, sb_ref, *rest):
    dl_refs, slab_ref = rest[:N_DIL], rest[N_DIL]
    tm = x_ref.shape[0]
    h = _layer_norm(x_ref[...], g_ref[...], b_ref[...])
    h_ref[...] = h
    hb = h.astype(BF16)
    gate_ref[...] = _dot(hb, wg_ref[...]).astype(BF16)
    sb_ref[...] = _dot(hb, wsb_ref[...]).astype(BF16)
    dl = _dot(hb, wdl_ref[...])
    slabs_per_group = DIL_GROUP_COLS // LANES
    for g, (_, dilation) in enumerate(DIL_PATTERNS):
        cols = dl[:, g * DIL_GROUP_COLS:(g + 1) * DIL_GROUP_COLS]
        if dilation == 1:
            dl_refs[g][0, 0] = cols.astype(BF16)
            continue
        for s in range(slabs_per_group):
            slab_ref[s] = cols[:, s * LANES:(s + 1) * LANES]
        for r in range(dilation):
            for s in range(slabs_per_group):
                rows = slab_ref[s, pl.ds(r, tm // dilation, stride=dilation), :]
                dl_refs[g][0, r, :, s * LANES:(s + 1) * LANES] = rows.astype(BF16)


def _inproj(x2, g, b, w_gate, w_sb, w_dl, batch):
    t = x2.shape[0]
    tm = INPROJ_TM
    s = t // batch
    tiles_per_seq = s // tm
    const = lambda a: pl.BlockSpec(a.shape, lambda i: (0,) * a.ndim, pipeline_mode=pl.Buffered(1))
    row = lambda w: pl.BlockSpec((tm, w), lambda i: (i, 0))
    dl_specs, dl_shapes = [], []
    for _, dilation in DIL_PATTERNS:
        dl_specs.append(pl.BlockSpec((1, dilation, tm // dilation, DIL_GROUP_COLS),
                                     lambda i: (i // tiles_per_seq, 0, i % tiles_per_seq, 0)))
        dl_shapes.append(jax.ShapeDtypeStruct((batch, dilation, s // dilation, DIL_GROUP_COLS), BF16))
    return pl.pallas_call(
        _inproj_body,
        grid=(t // tm,),
        in_specs=[row(D_MODEL), const(g), const(b), const(w_gate), const(w_sb), const(w_dl)],
        out_specs=[row(D_MODEL), row(GATE_COLS), row(SB_COLS)] + dl_specs,
        out_shape=[jax.ShapeDtypeStruct((t, D_MODEL), F32),
                   jax.ShapeDtypeStruct((t, GATE_COLS), BF16),
                   jax.ShapeDtypeStruct((t, SB_COLS), BF16)] + dl_shapes,
        scratch_shapes=[pltpu.VMEM((DIL_GROUP_COLS // LANES, tm, LANES), F32)],
        compiler_params=pltpu.CompilerParams(
            dimension_semantics=("parallel",), vmem_limit_bytes=VMEM_LIMIT),
        name="ln_inproj",
    )(x2, g, b, w_gate, w_sb, w_dl)


SB_TQ = 256


def _sb_body(q_ref, k_ref, v_ref, o_ref, *, tq, scale):
    i = pl.program_id(2)
    q2 = q_ref[0] * scale
    lane = lax.broadcasted_iota(I32, (tq, LANES), 1)
    zero = jnp.zeros_like(q2)
    q_a = jnp.where(lane < HEAD_DIM, q2, zero)
    q_b = jnp.where(lane >= HEAD_DIM, q2, zero)
    row = lax.broadcasted_iota(I32, (tq, tq), 0)
    col = lax.broadcasted_iota(I32, (tq, tq), 1)
    tri = col < row
    later = (row > col).astype(BF16)

    def one_head(qm, carry, acc, kb, vb, before):
        z = _dot_nt(qm, kb)
        softplus = jnp.maximum(z, 0.0) + jnp.log(1.0 + jnp.exp(-jnp.abs(z)))
        log_keep = jnp.where(before, -softplus, 0.0)
        lk_hi = log_keep.astype(BF16)
        lk_lo = (log_keep - lk_hi.astype(F32)).astype(BF16)
        between = _dot(lk_hi, later) + _dot(lk_lo, later)
        total = (z - softplus) + between + carry
        w = jnp.where(before, jnp.exp(total), 0.0)
        acc = acc + _dot(w.astype(BF16), vb)
        carry = carry + jnp.sum(log_keep, axis=-1, keepdims=True)
        return carry, acc

    def cond(st):
        return jnp.logical_and(st[0] >= 0, st[1] > 0)

    def body(st):
        j, _, c_a, c_b, a_a, a_b = st
        ks = pl.multiple_of(j * tq, tq)
        kb = k_ref[0, pl.ds(ks, tq), :]
        vb = v_ref[0, pl.ds(ks, tq), :]
        before = jnp.logical_or(tri, j < i)
        c_a, a_a = one_head(q_a, c_a, a_a, kb, vb, before)
        c_b, a_b = one_head(q_b, c_b, a_b, kb, vb, before)
        go = (jnp.max(jnp.maximum(c_a, c_b)) > SB_SKIP_LOG).astype(I32)
        return j - 1, go, c_a, c_b, a_a, a_b

    c0 = jnp.zeros((tq, 1), F32)
    a0 = jnp.zeros((tq, LANES), F32)
    st = lax.while_loop(cond, body, (i, jnp.int32(1), c0, c0, a0, a0))
    o_ref[0] = jnp.where(lane < HEAD_DIM, st[4], st[5]).astype(BF16)


def _sb_attention(proj3):
    b, s, _ = proj3.shape
    tq = SB_TQ
    qb, kb, vb = 0, SB_WIDTH // LANES, 2 * SB_WIDTH // LANES
    return pl.pallas_call(
        functools.partial(_sb_body, tq=tq, scale=1.0 / math.sqrt(HEAD_DIM)),
        grid=(b, SB_WIDTH // LANES, s // tq),
        in_specs=[
            pl.BlockSpec((1, tq, LANES), lambda bi, hp, i: (bi, i, qb + hp)),
            pl.BlockSpec((1, s, LANES), lambda bi, hp, i: (bi, 0, kb + hp)),
            pl.BlockSpec((1, s, LANES), lambda bi, hp, i: (bi, 0, vb + hp)),
        ],
        out_specs=pl.BlockSpec((1, tq, LANES), lambda bi, hp, i: (bi, i, hp)),
        out_shape=jax.ShapeDtypeStruct((b, s, SB_WIDTH), BF16),
        compiler_params=pltpu.CompilerParams(
            dimension_semantics=("parallel", "parallel", "arbitrary"), vmem_limit_bytes=VMEM_LIMIT),
        name="sb_attn",
    )(proj3, proj3, proj3)


DIL_TQ = 256
DIL_KP = DIL_WINDOW_KEYS


def _rel_bucket_idx(dist):
    max_exact = REL_BUCKETS // 2
    d = jnp.maximum(dist, 1).astype(F32)
    large = max_exact + (jnp.log(d / max_exact) / math.log(REL_MAX_DISTANCE / max_exact)
                         * (REL_BUCKETS - max_exact)).astype(I32)
    large = jnp.minimum(large, REL_BUCKETS - 1)
    return jnp.where(dist < max_exact, dist, large)


def _dil_bias_table(rel_bias, g, window, dilation):
    del window
    a = np.arange(DIL_TQ)[:, None]
    c = np.arange(DIL_KP + DIL_TQ)[None, :]
    step = a + DIL_KP - c
    valid = (step >= 0) & (step <= DIL_WINDOW_KEYS)
    bucket = _rel_bucket_idx(jnp.asarray(np.clip(step, 0, DIL_WINDOW_KEYS) * dilation, I32))
    heads = slice(g * DIL_HEADS_PER_GROUP, (g + 1) * DIL_HEADS_PER_GROUP)
    rb = rel_bias[:, heads].astype(F32)
    tab = jnp.zeros((DIL_HEADS_PER_GROUP,) + step.shape, F32)
    for bkt in range(REL_BUCKETS):
        tab = jnp.where(bucket[None] == bkt, rb[bkt][:, None, None], tab)
    return jnp.where(valid[None], tab, MASK_NEG)


def _dil_body(q_ref, kp_ref, kc_ref, vp_ref, vc_ref, bias_ref, o_ref, lse_ref, *, tq, scale):
    i = pl.program_id(2)
    q = q_ref[0]
    k2 = jnp.concatenate([kp_ref[0], kc_ref[0]], axis=0)
    v2 = jnp.concatenate([vp_ref[0], vc_ref[0]], axis=0)
    width = DIL_GROUP_WIDTH
    lane = lax.broadcasted_iota(I32, (tq, width), 1)
    keycol = lax.broadcasted_iota(I32, (tq, DIL_KP + tq), 1)
    no_prev = jnp.logical_and(i == 0, keycol < DIL_KP)
    zero = jnp.zeros_like(q)
    out = jnp.zeros((tq, width), F32)
    lse_out = jnp.zeros((tq, width), F32)
    for h in range(DIL_HEADS_PER_GROUP):
        in_head = jnp.logical_and(lane >= h * HEAD_DIM, lane < (h + 1) * HEAD_DIM)
        qm = jnp.where(in_head, q, zero)
        s = _dot_nt(qm, k2) * scale + bias_ref[h]
        s = jnp.where(no_prev, MASK_NEG, s)
        m = jnp.max(s, axis=-1, keepdims=True)
        p = jnp.exp(s - m)
        l = jnp.sum(p, axis=-1, keepdims=True)
        pv = _dot(p.astype(BF16), v2)
        out = jnp.where(in_head, pv / l, out)
        lse_out = jnp.where(in_head, m + jnp.log(l), lse_out)
    o_ref[0] = out
    lse_ref[0] = lse_out


def _dilated_group(qkv, bias_tab, g):
    b, dilation, sd, _ = qkv.shape
    tq = DIL_TQ
    width = DIL_GROUP_WIDTH
    blk = (1, None, tq, width)
    prev = (1, None, DIL_KP, width)
    per = tq // DIL_KP
    prev_row = lambda i: jnp.maximum(i * per - 1, 0)
    o, lse = pl.pallas_call(
        functools.partial(_dil_body, tq=tq, scale=1.0 / math.sqrt(HEAD_DIM)),
        grid=(b, dilation, sd // tq),
        in_specs=[
            pl.BlockSpec(blk, lambda bi, r, i: (bi, r, i, 0)),
            pl.BlockSpec(prev, lambda bi, r, i: (bi, r, prev_row(i), 1)),
            pl.BlockSpec(blk, lambda bi, r, i: (bi, r, i, 1)),
            pl.BlockSpec(prev, lambda bi, r, i: (bi, r, prev_row(i), 2)),
            pl.BlockSpec(blk, lambda bi, r, i: (bi, r, i, 2)),
            pl.BlockSpec((DIL_HEADS_PER_GROUP, tq, DIL_KP + tq), lambda bi, r, i: (0, 0, 0)),
        ],
        out_specs=[
            pl.BlockSpec((1, tq, width), lambda bi, r, i: (bi, i, r)),
            pl.BlockSpec((1, tq, width), lambda bi, r, i: (bi, i, r)),
        ],
        out_shape=[
            jax.ShapeDtypeStruct((b, sd, dilation * width), F32),
            jax.ShapeDtypeStruct((b, sd, dilation * width), F32),
        ],
        compiler_params=pltpu.CompilerParams(
            dimension_semantics=("parallel", "parallel", "arbitrary"), vmem_limit_bytes=VMEM_LIMIT),
        name=f"dil_attn_g{g}",
    )(qkv, qkv, qkv, qkv, qkv, bias_tab)
    t = b * dilation * sd
    return o.reshape(t, width), lse.reshape(t, width)


MERGE_TM = 256


def _merge_body(gp_ref, osb_ref, o0_ref, o1_ref, o2_ref, l0_ref, l1_ref, l2_ref, h_ref, bg_ref,
                wsb_ref, wdl_ref, wout_ref, g1_ref, b1_ref, wrh_ref, wrl_ref, h1_ref, h1p_ref, sc_ref):
    l0, l1, l2 = l0_ref[...], l1_ref[...], l2_ref[...]
    m = jnp.maximum(jnp.maximum(l0, l1), l2)
    e0, e1, e2 = jnp.exp(l0 - m), jnp.exp(l1 - m), jnp.exp(l2 - m)
    o_dl = (e0 * o0_ref[...] + e1 * o1_ref[...] + e2 * o2_ref[...]) / (e0 + e1 + e2)
    br_sb = _dot(osb_ref[...], wsb_ref[...])
    br_dl = _dot(o_dl.astype(BF16), wdl_ref[...])
    gates = jax.nn.sigmoid(gp_ref[...].astype(F32) + bg_ref[...])
    merged = gates[:, :D_MODEL] * br_sb + gates[:, D_MODEL:] * br_dl
    mix = _dot(merged.astype(BF16), wout_ref[...])
    h1 = _layer_norm(DN_ALPHA * h_ref[...] + mix, g1_ref[...], b1_ref[...])
    h1_ref[...] = h1
    h1p_ref[...] = _pack_rows(h1)
    h_hi = h1.astype(BF16)
    h_lo = (h1 - h_hi.astype(F32)).astype(BF16)
    logits = _dot_nt(wrh_ref[...], h_hi) + _dot_nt(wrh_ref[...], h_lo) + _dot_nt(wrl_ref[...], h_hi)
    sc_ref[...] = jax.nn.sigmoid(logits)


def _merge(gates, o_sb, dl, h, b_gate, w_br_sb, w_br_dil, w_out, g1, b1, wr_hi_t, wr_lo_t):
    t = h.shape[0]
    tm = MERGE_TM
    row = lambda w: pl.BlockSpec((tm, w), lambda i: (i, 0))
    full = lambda a: pl.BlockSpec(a.shape, lambda i: (0,) * a.ndim)
    args = (gates, o_sb, dl[0][0], dl[1][0], dl[2][0], dl[0][1], dl[1][1], dl[2][1], h, b_gate,
            w_br_sb, w_br_dil, w_out, g1, b1, wr_hi_t, wr_lo_t)
    in_specs = [row(GATE_COLS), row(SB_WIDTH)] + [row(DIL_GROUP_WIDTH)] * 6 + [row(D_MODEL)] \
        + [full(a) for a in args[9:]]
    return pl.pallas_call(
        _merge_body,
        grid=(t // tm,),
        in_specs=in_specs,
        out_specs=[row(D_MODEL), row(D_MODEL // 2), pl.BlockSpec((N_EXPERTS, tm), lambda i: (0, i))],
        out_shape=[jax.ShapeDtypeStruct((t, D_MODEL), F32), jax.ShapeDtypeStruct((t, D_MODEL // 2), U32),
                   jax.ShapeDtypeStruct((N_EXPERTS, t), F32)],
        compiler_params=pltpu.CompilerParams(
            dimension_semantics=("parallel",), vmem_limit_bytes=VMEM_LIMIT),
        name="merge_ln1",
    )(*args)


ROUTE_TR = 256


def _route_body(sc_ref, rb_ref, idx_ref, w_ref, rank_ref, cnt_ref, carry_ref, *, tr):
    @pl.when(pl.program_id(0) == 0)
    def _():
        carry_ref[...] = jnp.zeros_like(carry_ref)

    scores = sc_ref[...]
    biased = scores + rb_ref[...]
    erow = lax.broadcasted_iota(I32, (N_EXPERTS, tr), 0)
    neg = -jnp.inf

    def first_argmax(v, rows):
        m = jnp.max(v, axis=0, keepdims=True)
        return m, jnp.min(jnp.where(v == m, rows, N_EXPERTS), axis=0, keepdims=True)

    gslice = lambda a, g: a[g * GROUP_SIZE:(g + 1) * GROUP_SIZE, :]
    gscore = []
    grow = lax.broadcasted_iota(I32, (GROUP_SIZE, tr), 0)
    for g in range(N_GROUPS):
        vg, rg = gslice(biased, g), grow + g * GROUP_SIZE
        m1, i1 = first_argmax(vg, rg)
        m2 = jnp.max(jnp.where(rg == i1, neg, vg), axis=0, keepdims=True)
        gscore.append(m1 + m2)
    pieces = []
    for g in range(N_GROUPS):
        beaten = jnp.zeros((1, tr), I32)
        for g2 in range(N_GROUPS):
            if g2 == g:
                continue
            wins = gscore[g2] > gscore[g]
            if g2 < g:
                wins = jnp.logical_or(wins, gscore[g2] == gscore[g])
            beaten = beaten + wins.astype(I32)
        pieces.append(jnp.where(beaten < TOPK_GROUPS, gslice(biased, g), neg))
    masked = jnp.concatenate(pieces, axis=0)
    sel = jnp.zeros((N_EXPERTS, tr), jnp.bool_)
    picks = []
    for _ in range(TOP_K):
        _, ik = first_argmax(masked, erow)
        hit = erow == ik
        sel = jnp.logical_or(sel, hit)
        masked = jnp.where(hit, neg, masked)
        picks.append(ik)
    denom = jnp.sum(jnp.where(sel, scores, 0.0), axis=0, keepdims=True)

    r = lax.broadcasted_iota(I32, (tr, tr), 0)
    c = lax.broadcasted_iota(I32, (tr, tr), 1)
    earlier = (r < c).astype(BF16)
    sel_f = sel.astype(F32)
    before = _dot(sel_f.astype(BF16), earlier) + carry_ref[:, 0:1]
    new_carry = carry_ref[...] + jnp.sum(sel_f, axis=1, keepdims=True)
    carry_ref[...] = new_carry
    cnt_ref[...] = new_carry

    krow = lax.broadcasted_iota(I32, (TOP_K, tr), 0)
    idx_o = jnp.zeros((TOP_K, tr), I32)
    rank_o = jnp.zeros((TOP_K, tr), I32)
    w_o = jnp.zeros((TOP_K, tr), F32)
    for k, ik in enumerate(picks):
        hit = erow == ik
        wk = jnp.sum(jnp.where(hit, scores, 0.0), axis=0, keepdims=True) / denom * ROUTED_SCALE
        rk = jnp.sum(jnp.where(hit, before, 0.0), axis=0, keepdims=True).astype(I32)
        idx_o = jnp.where(krow == k, ik, idx_o)
        rank_o = jnp.where(krow == k, rk, rank_o)
        w_o = jnp.where(krow == k, wk, w_o)
    idx_ref[...] = idx_o
    rank_ref[...] = rank_o
    w_ref[...] = w_o


def _route(scores_t, router_bias_col):
    t = scores_t.shape[1]
    tr = ROUTE_TR
    col = lambda n: pl.BlockSpec((n, tr), lambda i: (0, i))
    return pl.pallas_call(
        functools.partial(_route_body, tr=tr),
        grid=(t // tr,),
        in_specs=[col(N_EXPERTS), pl.BlockSpec((N_EXPERTS, 1), lambda i: (0, 0))],
        out_specs=[col(TOP_K), col(TOP_K), col(TOP_K), pl.BlockSpec((N_EXPERTS, LANES), lambda i: (0, 0))],
        out_shape=[jax.ShapeDtypeStruct((TOP_K, t), I32), jax.ShapeDtypeStruct((TOP_K, t), F32),
                   jax.ShapeDtypeStruct((TOP_K, t), I32), jax.ShapeDtypeStruct((N_EXPERTS, LANES), F32)],
        scratch_shapes=[pltpu.VMEM((N_EXPERTS, LANES), F32)],
        compiler_params=pltpu.CompilerParams(
            dimension_semantics=("arbitrary",), vmem_limit_bytes=VMEM_LIMIT),
        name="route",
    )(scores_t, router_bias_col)


def _dest_body(idx_ref, rank_ref, ps_ref, dest_ref, *, tr):
    idx = idx_ref[...]
    erow = lax.broadcasted_iota(I32, (N_EXPERTS, tr), 0)
    starts = jnp.concatenate([ps_ref[...]] * (tr // LANES), axis=1)
    krow = lax.broadcasted_iota(I32, (TOP_K, tr), 0)
    out = rank_ref[...]
    for k in range(TOP_K):
        start_k = jnp.sum(jnp.where(erow == idx[k:k + 1, :], starts, 0), axis=0, keepdims=True)
        out = out + jnp.where(krow == k, start_k, 0)
    dest_ref[...] = out


def _dest(idx_t, rank_t, pstarts):
    t = idx_t.shape[1]
    tr = ROUTE_TR
    col = pl.BlockSpec((TOP_K, tr), lambda i: (0, i))
    starts = jnp.broadcast_to(pstarts[:, None], (N_EXPERTS, LANES))
    return pl.pallas_call(
        functools.partial(_dest_body, tr=tr),
        grid=(t // tr,),
        in_specs=[col, col, pl.BlockSpec((N_EXPERTS, LANES), lambda i: (0, 0))],
        out_specs=col,
        out_shape=jax.ShapeDtypeStruct((TOP_K, t), I32),
        compiler_params=pltpu.CompilerParams(
            dimension_semantics=("parallel",), vmem_limit_bytes=VMEM_LIMIT),
        name="dest",
    )(idx_t, rank_t, starts)


SC_WINDOW = 128
SC_COLS = 256
HALF = D_MODEL // 2


def _pack_rows(x):
    bits = lambda v: lax.bitcast_convert_type(v.astype(BF16).astype(F32), U32)
    return bits(x[:, HALF:]) | (bits(x[:, :HALF]) >> 16)


def _unpack_rows(w):
    lo = lax.bitcast_convert_type(w << 16, F32)
    hi = lax.bitcast_convert_type(w & jnp.uint32(0xFFFF0000), F32)
    return lo, hi


def _sc_mesh():
    return plsc.VectorSubcoreMesh(core_axis_name="core", subcore_axis_name="subcore")


def _sc_scatter_rows(x, dest_t, n_slots):
    t, width = x.shape

    @pl.kernel(out_type=jax.ShapeDtypeStruct((n_slots, width), x.dtype), mesh=_sc_mesh())
    def scatter(x_hbm, i_hbm, o_hbm):
        for k in range(TOP_K):
            for c in range(width // SC_COLS):
                def body(x_vmem, i_vmem, c=c):
                    pltpu.sync_copy(x_vmem, o_hbm.at[i_vmem.at[0], pl.ds(c * SC_COLS, SC_COLS)])

                pltpu.emit_pipeline(
                    body,
                    grid=(t // SC_WINDOW,),
                    in_specs=[pl.BlockSpec((SC_WINDOW, SC_COLS), index_map=lambda i, c=c: (i, c)),
                              pl.BlockSpec((1, SC_WINDOW), index_map=lambda i, k=k: (k, i))],
                    out_specs=[],
                    core_axis_name=("core", "subcore"),
                    dimension_semantics=(pltpu.PARALLEL,),
                )(x_hbm, i_hbm)

    return scatter(x, dest_t)


def _expert_body(be_ref, bv_ref, nu_ref, xs_ref, wg_ref, wu_ref, wd_ref, y_ref, wgb_ref, wub_ref, wdb_ref):
    i = pl.program_id(0)
    used = i < nu_ref[0]

    @pl.when(jnp.logical_and(used, jnp.logical_or(i == 0, be_ref[i] != be_ref[jnp.maximum(i - 1, 0)])))
    def _():
        wgb_ref[...] = wg_ref[0].astype(BF16)
        wub_ref[...] = wu_ref[0].astype(BF16)
        wdb_ref[...] = wd_ref[0].astype(BF16)

    @pl.when(used)
    def _():
        words = xs_ref[...]
        row = lax.broadcasted_iota(I32, words.shape, 0)
        lo, hi = _unpack_rows(jnp.where(row < bv_ref[i], words, jnp.uint32(0)))
        lo, hi = lo.astype(BF16), hi.astype(BF16)
        gate = _dot(lo, wgb_ref[:HALF, :]) + _dot(hi, wgb_ref[HALF:, :])
        up = _dot(lo, wub_ref[:HALF, :]) + _dot(hi, wub_ref[HALF:, :])
        y_ref[...] = _pack_rows(_dot((_silu(gate) * up).astype(BF16), wdb_ref[...]))

    @pl.when(jnp.logical_not(used))
    def _():
        y_ref[...] = jnp.zeros_like(y_ref)


def _experts(block_expert, block_valid, n_used, xs, w_gate_e, w_up_e, w_down_e):
    n_slots = xs.shape[0]
    bm = EXPERT_BM
    last = lambda i, nu: jnp.minimum(i, nu[0] - 1)
    return pl.pallas_call(
        _expert_body,
        grid_spec=pltpu.PrefetchScalarGridSpec(
            num_scalar_prefetch=3,
            grid=(n_slots // bm,),
            in_specs=[
                pl.BlockSpec((bm, HALF), lambda i, be, bv, nu: (last(i, nu), 0)),
                pl.BlockSpec((1, D_MODEL, EXPERT_HIDDEN), lambda i, be, bv, nu: (be[last(i, nu)], 0, 0)),
                pl.BlockSpec((1, D_MODEL, EXPERT_HIDDEN), lambda i, be, bv, nu: (be[last(i, nu)], 0, 0)),
                pl.BlockSpec((1, EXPERT_HIDDEN, D_MODEL), lambda i, be, bv, nu: (be[last(i, nu)], 0, 0)),
            ],
            out_specs=pl.BlockSpec((bm, HALF), lambda i, be, bv, nu: (i, 0)),
            scratch_shapes=[pltpu.VMEM((D_MODEL, EXPERT_HIDDEN), BF16),
                            pltpu.VMEM((D_MODEL, EXPERT_HIDDEN), BF16),
                            pltpu.VMEM((EXPERT_HIDDEN, D_MODEL), BF16)],
        ),
        out_shape=jax.ShapeDtypeStruct((n_slots, HALF), U32),
        compiler_params=pltpu.CompilerParams(
            dimension_semantics=("arbitrary",), vmem_limit_bytes=VMEM_LIMIT),
        name="experts",
    )(block_expert, block_valid, n_used, xs, w_gate_e, w_up_e, w_down_e)


def _sc_gather_rows(table, idx_row):
    m = idx_row.shape[1]
    width = table.shape[1]

    @pl.kernel(out_type=jax.ShapeDtypeStruct((m, width), table.dtype), mesh=_sc_mesh())
    def gather(x_hbm, i_hbm, o_hbm):
        for c in range(width // SC_COLS):
            def body(i_vmem, o_vmem, c=c):
                pltpu.sync_copy(x_hbm.at[i_vmem.at[0], pl.ds(c * SC_COLS, SC_COLS)], o_vmem)

            pltpu.emit_pipeline(
                body,
                grid=(m // SC_WINDOW,),
                in_specs=[pl.BlockSpec((1, SC_WINDOW), index_map=lambda i: (0, i))],
                out_specs=[pl.BlockSpec((SC_WINDOW, SC_COLS), index_map=lambda i, c=c: (i, c))],
                core_axis_name=("core", "subcore"),
                dimension_semantics=(pltpu.PARALLEL,),
            )(i_hbm, o_hbm)

    return gather(table, idx_row)


COMBINE_TC = 256


def _combine_body(w_ref, h1_ref, *rest):
    rows = rest[:TOP_K]
    wgs_ref, wus_ref, wds_ref, g2_ref, b2_ref, o_ref = rest[TOP_K:]
    h1 = h1_ref[...]
    hb = h1.astype(BF16)
    shared = _dot((_silu(_dot(hb, wgs_ref[...])) * _dot(hb, wus_ref[...])).astype(BF16), wds_ref[...])
    w = w_ref[...]
    acc_lo = jnp.zeros((h1.shape[0], HALF), F32)
    acc_hi = jnp.zeros((h1.shape[0], HALF), F32)
    for k in range(TOP_K):
        lo, hi = _unpack_rows(rows[k][...])
        acc_lo = acc_lo + w[:, k:k + 1] * lo
        acc_hi = acc_hi + w[:, k:k + 1] * hi
    routed = jnp.concatenate([acc_lo, acc_hi], axis=1)
    o_ref[...] = _layer_norm(DN_ALPHA * h1 + routed + shared, g2_ref[...], b2_ref[...])


def _combine(w_tok, h1, gathered, wgs, wus, wds, g2, b2):
    t = h1.shape[0]
    tc = COMBINE_TC
    n_tiles = t // tc
    full = lambda a: pl.BlockSpec(a.shape, lambda i: (0,) * a.ndim)
    row_specs = [pl.BlockSpec((tc, HALF), lambda i, k=k: (k * n_tiles + i, 0)) for k in range(TOP_K)]
    return pl.pallas_call(
        _combine_body,
        grid=(n_tiles,),
        in_specs=[pl.BlockSpec((tc, TOP_K), lambda i: (i, 0)),
                  pl.BlockSpec((tc, D_MODEL), lambda i: (i, 0))] + row_specs
        + [full(wgs), full(wus), full(wds), full(g2), full(b2)],
        out_specs=pl.BlockSpec((tc, D_MODEL), lambda i: (i, 0)),
        out_shape=jax.ShapeDtypeStruct((t, D_MODEL), F32),
        compiler_params=pltpu.CompilerParams(
            dimension_semantics=("parallel",), vmem_limit_bytes=VMEM_LIMIT),
        name="combine_ln2",
    )(w_tok, h1, *([gathered] * TOP_K), wgs, wus, wds, g2, b2)


def _split_w_in(w):
    w = w.astype(BF16)
    sb = w[:, :SB_COLS]
    dq, dk, dv = (w[:, SB_COLS + j * DIL_WIDTH:SB_COLS + (j + 1) * DIL_WIDTH] for j in range(3))
    grp = lambda m, g: m[:, g * DIL_GROUP_WIDTH:(g + 1) * DIL_GROUP_WIDTH]
    dl = jnp.concatenate([grp(m, g) for g in range(N_DIL) for m in (dq, dk, dv)], axis=1)
    gates = w[:, SB_COLS + 3 * DIL_WIDTH:]
    return gates, sb, dl


def _moe(h1, h1_packed, scores_t, router_bias, w_gate_e, w_up_e, w_down_e, w_gate_s, w_up_s, w_down_s, g2, b2):
    t = h1.shape[0]
    idx_t, w_t, rank_t, counts = _route(scores_t, router_bias[:, None])
    counts = counts[:, 0].astype(I32)
    padded = (counts + EXPERT_BM - 1) // EXPERT_BM * EXPERT_BM
    pends = jnp.cumsum(padded)
    pstarts = (pends - padded).astype(I32)
    n_blocks = (t * TOP_K) // EXPERT_BM + N_EXPERTS
    block_start = jnp.arange(n_blocks, dtype=I32) * EXPERT_BM
    block_expert = jnp.minimum(jnp.sum((pends[None, :] <= block_start[:, None]).astype(I32), axis=1),
                               N_EXPERTS - 1).astype(I32)
    n_used = (pends[-1:] // EXPERT_BM).astype(I32)
    block_valid = jnp.clip(counts[block_expert] + pstarts[block_expert] - block_start, 0, EXPERT_BM).astype(I32)
    dest_t = _dest(idx_t, rank_t, pstarts)
    xs = _sc_scatter_rows(h1_packed, dest_t, n_blocks * EXPERT_BM)
    ys = _experts(block_expert, block_valid, n_used, xs, w_gate_e, w_up_e, w_down_e)
    gathered = _sc_gather_rows(ys, dest_t.reshape(1, TOP_K * t))
    return _combine(w_t.T, h1, gathered,
                    w_gate_s.astype(BF16), w_up_s.astype(BF16), w_down_s.astype(BF16),
                    g2[None], b2[None])


def kernel(x, ln_in_g, ln_in_b, rel_bias, w_in, b_gate, w_br_sb, w_br_dil, w_out, ln1_g, ln1_b,
           w_router, router_bias, w_gate_e, w_up_e, w_down_e, w_gate_s, w_up_s, w_down_s,
           ln2_g, ln2_b):
    b, s, d = x.shape
    t = b * s
    n_layers = w_in.shape[0]
    h = x.reshape(t, d)
    ln_g, ln_b = ln_in_g[None], ln_in_b[None]
    bias_tabs = [_dil_bias_table(rel_bias, g, window, dil) for g, (window, dil) in enumerate(DIL_PATTERNS)]
    for l in range(n_layers):
        assert l == 0, "single-layer block"
        w_gate, w_sb, w_dl = _split_w_in(w_in[l])
        h, gates, sb, *dl_qkv = _inproj(h, ln_g, ln_b, w_gate, w_sb, w_dl, b)
        o_sb = _sb_attention(sb.reshape(b, s, SB_COLS)).reshape(t, SB_WIDTH)
        dl = [_dilated_group(dl_qkv[g], bias_tabs[g], g) for g in range(N_DIL)]
        wr = w_router[l].T
        wr_hi = wr.astype(BF16)
        wr_lo = (wr - wr_hi.astype(F32)).astype(BF16)
        h1, h1_packed, scores = _merge(gates, o_sb, dl, h, b_gate[l][None], w_br_sb[l].astype(BF16),
                                       w_br_dil[l].astype(BF16), w_out[l].astype(BF16),
                                       ln1_g[l][None], ln1_b[l][None], wr_hi, wr_lo)
        h = _moe(h1, h1_packed, scores, router_bias[l], w_gate_e[l], w_up_e[l], w_down_e[l],
                 w_gate_s[l], w_up_s[l], w_down_s[l], ln2_g[l], ln2_b[l])
    return h.reshape(b, s, d)
```

```python
import functools
import math

import numpy as np
import jax
import jax.numpy as jnp
from jax import lax
from jax.experimental import pallas as pl
from jax.experimental.pallas import tpu as pltpu
from jax.experimental.pallas import tpu_sc as plsc

F32 = jnp.float32
BF16 = jnp.bfloat16
I32 = jnp.int32
U32 = jnp.uint32

D_MODEL = 1024
HEAD_DIM = 64
SB_HEADS = 8
DIL_PATTERNS = ((128, 1), (512, 4), (2048, 16))
DIL_HEADS_PER_GROUP = 4
DIL_WINDOW_KEYS = 128
REL_BUCKETS = 32
REL_MAX_DISTANCE = 2048
N_EXPERTS = 256
TOP_K = 8
N_GROUPS = 8
GROUP_SIZE = N_EXPERTS // N_GROUPS
TOPK_GROUPS = 4
EXPERT_HIDDEN = 256
ROUTED_SCALE = 2.5
LN_EPS = 1e-5
DEPTH = 1
DN_ALPHA = (2 * DEPTH) ** 0.25

SB_WIDTH = SB_HEADS * HEAD_DIM
DIL_GROUP_WIDTH = DIL_HEADS_PER_GROUP * HEAD_DIM
DIL_WIDTH = DIL_GROUP_WIDTH * len(DIL_PATTERNS)

GATE_COLS = 2 * D_MODEL
SB_COLS = 3 * SB_WIDTH
DIL_GROUP_COLS = 3 * DIL_GROUP_WIDTH
N_DIL = len(DIL_PATTERNS)

LANES = 128
SUBLANES = 8
EXPERT_BM = 256
MASK_NEG = -1e30
SB_SKIP_LOG = -104.0

VMEM_LIMIT = 48 * 1024 * 1024


def _layer_norm(x, g, b):
    mu = jnp.mean(x, axis=-1, keepdims=True)
    xc = x - mu
    var = jnp.mean(xc * xc, axis=-1, keepdims=True)
    return xc * lax.rsqrt(var + LN_EPS) * g + b


def _dot(a, b):
    return jnp.dot(a, b, preferred_element_type=F32)


def _dot_nt(a, b):
    return lax.dot_general(a, b, (((1,), (1,)), ((), ())), preferred_element_type=F32)


def _silu(x):
    return x * jax.nn.sigmoid(x)


INPROJ_TM = 512


def _inproj_body(x_ref, g_ref, b_ref, wg_ref, wsb_ref, wdl_ref, h_ref, gate_ref, sb_ref, *rest):
    dl_refs, slab_ref = rest[:N_DIL], rest[N_DIL]
    tm = x_ref.shape[0]
    h = _layer_norm(x_ref[...], g_ref[...], b_ref[...])
    h_ref[...] = h
    hb = h.astype(BF16)
    gate_ref[...] = _dot(hb, wg_ref[...]).astype(BF16)
    sb_ref[...] = _dot(hb, wsb_ref[...]).astype(BF16)
    dl = _dot(hb, wdl_ref[...])
    slabs_per_group = DIL_GROUP_COLS // LANES
    for g, (_, dilation) in enumerate(DIL_PATTERNS):
        cols = dl[:, g * DIL_GROUP_COLS:(g + 1) * DIL_GROUP_COLS]
        if dilation == 1:
            dl_refs[g][0, 0] = cols.astype(BF16)
            continue
        for s in range(slabs_per_group):
            slab_ref[s] = cols[:, s * LANES:(s + 1) * LANES]
        for r in range(dilation):
            for s in range(slabs_per_group):
                rows = slab_ref[s, pl.ds(r, tm // dilation, stride=dilation), :]
                dl_refs[g][0, r, :, s * LANES:(s + 1) * LANES] = rows.astype(BF16)


def _inproj(x2, g, b, w_gate, w_sb, w_dl, batch):
    t = x2.shape[0]
    tm = INPROJ_TM
    s = t // batch
    tiles_per_seq = s // tm
    const = lambda a: pl.BlockSpec(a.shape, lambda i: (0,) * a.ndim, pipeline_mode=pl.Buffered(1))
    row = lambda w: pl.BlockSpec((tm, w), lambda i: (i, 0))
    dl_specs, dl_shapes = [], []
    for _, dilation in DIL_PATTERNS:
        dl_specs.append(pl.BlockSpec((1, dilation, tm // dilation, DIL_GROUP_COLS),
                                     lambda i: (i // tiles_per_seq, 0, i % tiles_per_seq, 0)))
        dl_shapes.append(jax.ShapeDtypeStruct((batch, dilation, s // dilation, DIL_GROUP_COLS), BF16))
    return pl.pallas_call(
        _inproj_body,
        grid=(t // tm,),
        in_specs=[row(D_MODEL), const(g), const(b), const(w_gate), const(w_sb), const(w_dl)],
        out_specs=[row(D_MODEL), row(GATE_COLS), row(SB_COLS)] + dl_specs,
        out_shape=[jax.ShapeDtypeStruct((t, D_MODEL), F32),
                   jax.ShapeDtypeStruct((t, GATE_COLS), BF16),
                   jax.ShapeDtypeStruct((t, SB_COLS), BF16)] + dl_shapes,
        scratch_shapes=[pltpu.VMEM((DIL_GROUP_COLS // LANES, tm, LANES), F32)],
        compiler_params=pltpu.CompilerParams(
            dimension_semantics=("parallel",), vmem_limit_bytes=VMEM_LIMIT),
        name="ln_inproj",
    )(x2, g, b, w_gate, w_sb, w_dl)


SB_TQ = 256


def _sb_body(q_ref, k_ref, v_ref, o_ref, *, tq, scale):
    i = pl.program_id(2)
    q2 = q_ref[0] * scale
    lane = lax.broadcasted_iota(I32, (tq, LANES), 1)
    zero = jnp.zeros_like(q2)
    q_a = jnp.where(lane < HEAD_DIM, q2, zero)
    q_b = jnp.where(lane >= HEAD_DIM, q2, zero)
    row = lax.broadcasted_iota(I32, (tq, tq), 0)
    col = lax.broadcasted_iota(I32, (tq, tq), 1)
    tri = col < row
    later = (row > col).astype(BF16)

    def one_head(qm, carry, acc, kb, vb, before):
        z = _dot_nt(qm, kb)
        softplus = jnp.maximum(z, 0.0) + jnp.log(1.0 + jnp.exp(-jnp.abs(z)))
        log_keep = jnp.where(before, -softplus, 0.0)
        lk_hi = log_keep.astype(BF16)
        lk_lo = (log_keep - lk_hi.astype(F32)).astype(BF16)
        between = _dot(lk_hi, later) + _dot(lk_lo, later)
        total = (z - softplus) + between + carry
        w = jnp.where(before, jnp.exp(total), 0.0)
        acc = acc + _dot(w.astype(BF16), vb)
        carry = carry + jnp.sum(log_keep, axis=-1, keepdims=True)
        return carry, acc

    def cond(st):
        return jnp.logical_and(st[0] >= 0, st[1] > 0)

    def body(st):
        j, _, c_a, c_b, a_a, a_b = st
        ks = pl.multiple_of(j * tq, tq)
        kb = k_ref[0, pl.ds(ks, tq), :]
        vb = v_ref[0, pl.ds(ks, tq), :]
        before = jnp.logical_or(tri, j < i)
        c_a, a_a = one_head(q_a, c_a, a_a, kb, vb, before)
        c_b, a_b = one_head(q_b, c_b, a_b, kb, vb, before)
        go = (jnp.max(jnp.maximum(c_a, c_b)) > SB_SKIP_LOG).astype(I32)
        return j - 1, go, c_a, c_b, a_a, a_b

    c0 = jnp.zeros((tq, 1), F32)
    a0 = jnp.zeros((tq, LANES), F32)
    st = lax.while_loop(cond, body, (i, jnp.int32(1), c0, c0, a0, a0))
    o_ref[0] = jnp.where(lane < HEAD_DIM, st[4], st[5]).astype(BF16)


def _sb_attention(proj3):
    b, s, _ = proj3.shape
    tq = SB_TQ
    qb, kb, vb = 0, SB_WIDTH // LANES, 2 * SB_WIDTH // LANES
    return pl.pallas_call(
        functools.partial(_sb_body, tq=tq, scale=1.0 / math.sqrt(HEAD_DIM)),
        grid=(b, SB_WIDTH // LANES, s // tq),
        in_specs=[
            pl.BlockSpec((1, tq, LANES), lambda bi, hp, i: (bi, i, qb + hp)),
            pl.BlockSpec((1, s, LANES), lambda bi, hp, i: (bi, 0, kb + hp)),
            pl.BlockSpec((1, s, LANES), lambda bi, hp, i: (bi, 0, vb + hp)),
        ],
        out_specs=pl.BlockSpec((1, tq, LANES), lambda bi, hp, i: (bi, i, hp)),
        out_shape=jax.ShapeDtypeStruct((b, s, SB_WIDTH), BF16),
        compiler_params=pltpu.CompilerParams(
            dimension_semantics=("parallel", "parallel", "arbitrary"), vmem_limit_bytes=VMEM_LIMIT),
        name="sb_attn",
    )(proj3, proj3, proj3)


DIL_TQ = 256
DIL_KP = DIL_WINDOW_KEYS


def _rel_bucket_idx(dist):
    max_exact = REL_BUCKETS // 2
    d = jnp.maximum(dist, 1).astype(F32)
    large = max_exact + (jnp.log(d / max_exact) / math.log(REL_MAX_DISTANCE / max_exact)
                         * (REL_BUCKETS - max_exact)).astype(I32)
    large = jnp.minimum(large, REL_BUCKETS - 1)
    return jnp.where(dist < max_exact, dist, large)


def _dil_bias_table(rel_bias, g, window, dilation):
    del window
    a = np.arange(DIL_TQ)[:, None]
    c = np.arange(DIL_KP + DIL_TQ)[None, :]
    step = a + DIL_KP - c
    valid = (step >= 0) & (step <= DIL_WINDOW_KEYS)
    bucket = _rel_bucket_idx(jnp.asarray(np.clip(step, 0, DIL_WINDOW_KEYS) * dilation, I32))
    heads = slice(g * DIL_HEADS_PER_GROUP, (g + 1) * DIL_HEADS_PER_GROUP)
    rb = rel_bias[:, heads].astype(F32)
    tab = jnp.zeros((DIL_HEADS_PER_GROUP,) + step.shape, F32)
    for bkt in range(REL_BUCKETS):
        tab = jnp.where(bucket[None] == bkt, rb[bkt][:, None, None], tab)
    return jnp.where(valid[None], tab, MASK_NEG)


def _dil_body(q_ref, kp_ref, kc_ref, vp_ref, vc_ref, bias_ref, o_ref, lse_ref, *, tq, scale):
    i = pl.program_id(2)
    q = q_ref[0]
    k2 = jnp.concatenate([kp_ref[0], kc_ref[0]], axis=0)
    v2 = jnp.concatenate([vp_ref[0], vc_ref[0]], axis=0)
    width = DIL_GROUP_WIDTH
    lane = lax.broadcasted_iota(I32, (tq, width), 1)
    keycol = lax.broadcasted_iota(I32, (tq, DIL_KP + tq), 1)
    no_prev = jnp.logical_and(i == 0, keycol < DIL_KP)
    zero = jnp.zeros_like(q)
    out = jnp.zeros((tq, width), F32)
    lse_out = jnp.zeros((tq, width), F32)
    for h in range(DIL_HEADS_PER_GROUP):
        in_head = jnp.logical_and(lane >= h * HEAD_DIM, lane < (h + 1) * HEAD_DIM)
        qm = jnp.where(in_head, q, zero)
        s = _dot_nt(qm, k2) * scale + bias_ref[h]
        s = jnp.where(no_prev, MASK_NEG, s)
        m = jnp.max(s, axis=-1, keepdims=True)
        p = jnp.exp(s - m)
        l = jnp.sum(p, axis=-1, keepdims=True)
        pv = _dot(p.astype(BF16), v2)
        out = jnp.where(in_head, pv / l, out)
        lse_out = jnp.where(in_head, m + jnp.log(l), lse_out)
    o_ref[0] = out
    lse_ref[0] = lse_out


def _dilated_group(qkv, bias_tab, g):
    b, dilation, sd, _ = qkv.shape
    tq = DIL_TQ
    width = DIL_GROUP_WIDTH
    blk = (1, None, tq, width)
    prev = (1, None, DIL_KP, width)
    per = tq // DIL_KP
    prev_row = lambda i: jnp.maximum(i * per - 1, 0)
    o, lse = pl.pallas_call(
        functools.partial(_dil_body, tq=tq, scale=1.0 / math.sqrt(HEAD_DIM)),
        grid=(b, dilation, sd // tq),
        in_specs=[
            pl.BlockSpec(blk, lambda bi, r, i: (bi, r, i, 0)),
            pl.BlockSpec(prev, lambda bi, r, i: (bi, r, prev_row(i), 1)),
            pl.BlockSpec(blk, lambda bi, r, i: (bi, r, i, 1)),
            pl.BlockSpec(prev, lambda bi, r, i: (bi, r, prev_row(i), 2)),
            pl.BlockSpec(blk, lambda bi, r, i: (bi, r, i, 2)),
            pl.BlockSpec((DIL_HEADS_PER_GROUP, tq, DIL_KP + tq), lambda bi, r, i: (0, 0, 0)),
        ],
        out_specs=[
            pl.BlockSpec((1, tq, width), lambda bi, r, i: (bi, i, r)),
            pl.BlockSpec((1, tq, width), lambda bi, r, i: (bi, i, r)),
        ],
        out_shape=[
            jax.ShapeDtypeStruct((b, sd, dilation * width), F32),
            jax.ShapeDtypeStruct((b, sd, dilation * width), F32),
        ],
        compiler_params=pltpu.CompilerParams(
            dimension_semantics=("parallel", "parallel", "arbitrary"), vmem_limit_bytes=VMEM_LIMIT),
        name=f"dil_attn_g{g}",
    )(qkv, qkv, qkv, qkv, qkv, bias_tab)
    t = b * dilation * sd
    return o.reshape(t, width), lse.reshape(t, width)


MERGE_TM = 256


def _merge_body(gp_ref, osb_ref, o0_ref, o1_ref, o2_ref, l0_ref, l1_ref, l2_ref, h_ref, bg_ref,
                wsb_ref, wdl_ref, wout_ref, g1_ref, b1_ref, wrh_ref, wrl_ref, h1_ref, h1p_ref, sc_ref):
    l0, l1, l2 = l0_ref[...], l1_ref[...], l2_ref[...]
    m = jnp.maximum(jnp.maximum(l0, l1), l2)
    e0, e1, e2 = jnp.exp(l0 - m), jnp.exp(l1 - m), jnp.exp(l2 - m)
    o_dl = (e0 * o0_ref[...] + e1 * o1_ref[...] + e2 * o2_ref[...]) / (e0 + e1 + e2)
    br_sb = _dot(osb_ref[...], wsb_ref[...])
    br_dl = _dot(o_dl.astype(BF16), wdl_ref[...])
    gates = jax.nn.sigmoid(gp_ref[...].astype(F32) + bg_ref[...])
    merged = gates[:, :D_MODEL] * br_sb + gates[:, D_MODEL:] * br_dl
    mix = _dot(merged.astype(BF16), wout_ref[...])
    h1 = _layer_norm(DN_ALPHA * h_ref[...] + mix, g1_ref[...], b1_ref[...])
    h1_ref[...] = h1
    h1p_ref[...] = _pack_rows(h1)
    h_hi = h1.astype(BF16)
    h_lo = (h1 - h_hi.astype(F32)).astype(BF16)
    logits = _dot_nt(wrh_ref[...], h_hi) + _dot_nt(wrh_ref[...], h_lo) + _dot_nt(wrl_ref[...], h_hi)
    sc_ref[...] = jax.nn.sigmoid(logits)


def _merge(gates, o_sb, dl, h, b_gate, w_br_sb, w_br_dil, w_out, g1, b1, wr_hi_t, wr_lo_t):
    t = h.shape[0]
    tm = MERGE_TM
    row = lambda w: pl.BlockSpec((tm, w), lambda i: (i, 0))
    full = lambda a: pl.BlockSpec(a.shape, lambda i: (0,) * a.ndim)
    args = (gates, o_sb, dl[0][0], dl[1][0], dl[2][0], dl[0][1], dl[1][1], dl[2][1], h, b_gate,
            w_br_sb, w_br_dil, w_out, g1, b1, wr_hi_t, wr_lo_t)
    in_specs = [row(GATE_COLS), row(SB_WIDTH)] + [row(DIL_GROUP_WIDTH)] * 6 + [row(D_MODEL)] \
        + [full(a) for a in args[9:]]
    return pl.pallas_call(
        _merge_body,
        grid=(t // tm,),
        in_specs=in_specs,
        out_specs=[row(D_MODEL), row(D_MODEL // 2), pl.BlockSpec((N_EXPERTS, tm), lambda i: (0, i))],
        out_shape=[jax.ShapeDtypeStruct((t, D_MODEL), F32), jax.ShapeDtypeStruct((t, D_MODEL // 2), U32),
                   jax.ShapeDtypeStruct((N_EXPERTS, t), F32)],
        compiler_params=pltpu.CompilerParams(
            dimension_semantics=("parallel",), vmem_limit_bytes=VMEM_LIMIT),
        name="merge_ln1",
    )(*args)


ROUTE_TR = 256


def _route_body(sc_ref, rb_ref, idx_ref, w_ref, rank_ref, cnt_ref, carry_ref, *, tr):
    @pl.when(pl.program_id(0) == 0)
    def _():
        carry_ref[...] = jnp.zeros_like(carry_ref)

    scores = sc_ref[...]
    biased = scores + rb_ref[...]
    erow = lax.broadcasted_iota(I32, (N_EXPERTS, tr), 0)
    neg = -jnp.inf

    def first_argmax(v, rows):
        m = jnp.max(v, axis=0, keepdims=True)
        return m, jnp.min(jnp.where(v == m, rows, N_EXPERTS), axis=0, keepdims=True)

    gslice = lambda a, g: a[g * GROUP_SIZE:(g + 1) * GROUP_SIZE, :]
    gscore = []
    grow = lax.broadcasted_iota(I32, (GROUP_SIZE, tr), 0)
    for g in range(N_GROUPS):
        vg, rg = gslice(biased, g), grow + g * GROUP_SIZE
        m1, i1 = first_argmax(vg, rg)
        m2 = jnp.max(jnp.where(rg == i1, neg, vg), axis=0, keepdims=True)
        gscore.append(m1 + m2)
    pieces = []
    for g in range(N_GROUPS):
        beaten = jnp.zeros((1, tr), I32)
        for g2 in range(N_GROUPS):
            if g2 == g:
                continue
            wins = gscore[g2] > gscore[g]
            if g2 < g:
                wins = jnp.logical_or(wins, gscore[g2] == gscore[g])
            beaten = beaten + wins.astype(I32)
        pieces.append(jnp.where(beaten < TOPK_GROUPS, gslice(biased, g), neg))
    masked = jnp.concatenate(pieces, axis=0)
    sel = jnp.zeros((N_EXPERTS, tr), jnp.bool_)
    picks = []
    for _ in range(TOP_K):
        _, ik = first_argmax(masked, erow)
        hit = erow == ik
        sel = jnp.logical_or(sel, hit)
        masked = jnp.where(hit, neg, masked)
        picks.append(ik)
    denom = jnp.sum(jnp.where(sel, scores, 0.0), axis=0, keepdims=True)

    r = lax.broadcasted_iota(I32, (tr, tr), 0)
    c = lax.broadcasted_iota(I32, (tr, tr), 1)
    earlier = (r < c).astype(BF16)
    sel_f = sel.astype(F32)
    before = _dot(sel_f.astype(BF16), earlier) + carry_ref[:, 0:1]
    new_carry = carry_ref[...] + jnp.sum(sel_f, axis=1, keepdims=True)
    carry_ref[...] = new_carry
    cnt_ref[...] = new_carry

    krow = lax.broadcasted_iota(I32, (TOP_K, tr), 0)
    idx_o = jnp.zeros((TOP_K, tr), I32)
    rank_o = jnp.zeros((TOP_K, tr), I32)
    w_o = jnp.zeros((TOP_K, tr), F32)
    for k, ik in enumerate(picks):
        hit = erow == ik
        wk = jnp.sum(jnp.where(hit, scores, 0.0), axis=0, keepdims=True) / denom * ROUTED_SCALE
        rk = jnp.sum(jnp.where(hit, before, 0.0), axis=0, keepdims=True).astype(I32)
        idx_o = jnp.where(krow == k, ik, idx_o)
        rank_o = jnp.where(krow == k, rk, rank_o)
        w_o = jnp.where(krow == k, wk, w_o)
    idx_ref[...] = idx_o
    rank_ref[...] = rank_o
    w_ref[...] = w_o


def _route(scores_t, router_bias_col):
    t = scores_t.shape[1]
    tr = ROUTE_TR
    col = lambda n: pl.BlockSpec((n, tr), lambda i: (0, i))
    return pl.pallas_call(
        functools.partial(_route_body, tr=tr),
        grid=(t // tr,),
        in_specs=[col(N_EXPERTS), pl.BlockSpec((N_EXPERTS, 1), lambda i: (0, 0))],
        out_specs=[col(TOP_K), col(TOP_K), col(TOP_K), pl.BlockSpec((N_EXPERTS, LANES), lambda i: (0, 0))],
        out_shape=[jax.ShapeDtypeStruct((TOP_K, t), I32), jax.ShapeDtypeStruct((TOP_K, t), F32),
                   jax.ShapeDtypeStruct((TOP_K, t), I32), jax.ShapeDtypeStruct((N_EXPERTS, LANES), F32)],
        scratch_shapes=[pltpu.VMEM((N_EXPERTS, LANES), F32)],
        compiler_params=pltpu.CompilerParams(
            dimension_semantics=("arbitrary",), vmem_limit_bytes=VMEM_LIMIT),
        name="route",
    )(scores_t, router_bias_col)


def _dest_body(idx_ref, rank_ref, ps_ref, dest_ref, *, tr):
    idx = idx_ref[...]
    erow = lax.broadcasted_iota(I32, (N_EXPERTS, tr), 0)
    starts = jnp.concatenate([ps_ref[...]] * (tr // LANES), axis=1)
    krow = lax.broadcasted_iota(I32, (TOP_K, tr), 0)
    out = rank_ref[...]
    for k in range(TOP_K):
        start_k = jnp.sum(jnp.where(erow == idx[k:k + 1, :], starts, 0), axis=0, keepdims=True)
        out = out + jnp.where(krow == k, start_k, 0)
    dest_ref[...] = out


def _dest(idx_t, rank_t, pstarts):
    t = idx_t.shape[1]
    tr = ROUTE_TR
    col = pl.BlockSpec((TOP_K, tr), lambda i: (0, i))
    starts = jnp.broadcast_to(pstarts[:, None], (N_EXPERTS, LANES))
    return pl.pallas_call(
        functools.partial(_dest_body, tr=tr),
        grid=(t // tr,),
        in_specs=[col, col, pl.BlockSpec((N_EXPERTS, LANES), lambda i: (0, 0))],
        out_specs=col,
        out_shape=jax.ShapeDtypeStruct((TOP_K, t), I32),
        compiler_params=pltpu.CompilerParams(
            dimension_semantics=("parallel",), vmem_limit_bytes=VMEM_LIMIT),
        name="dest",
    )(idx_t, rank_t, starts)


SC_WINDOW = 128
SC_COLS = 256
HALF = D_MODEL // 2


def _pack_rows(x):
    bits = lambda v: lax.bitcast_convert_type(v.astype(BF16).astype(F32), U32)
    return bits(x[:, HALF:]) | (bits(x[:, :HALF]) >> 16)


def _unpack_rows(w):
    lo = lax.bitcast_convert_type(w << 16, F32)
    hi = lax.bitcast_convert_type(w & jnp.uint32(0xFFFF0000), F32)
    return lo, hi


def _sc_mesh():
    return plsc.VectorSubcoreMesh(core_axis_name="core", subcore_axis_name="subcore")


def _sc_scatter_rows(x, dest_t, n_slots):
    t, width = x.shape

    @pl.kernel(out_type=jax.ShapeDtypeStruct((n_slots, width), x.dtype), mesh=_sc_mesh())
    def scatter(x_hbm, i_hbm, o_hbm):
        for k in range(TOP_K):
            for c in range(width // SC_COLS):
                def body(x_vmem, i_vmem, c=c):
                    pltpu.sync_copy(x_vmem, o_hbm.at[i_vmem.at[0], pl.ds(c * SC_COLS, SC_COLS)])

                pltpu.emit_pipeline(
                    body,
                    grid=(t // SC_WINDOW,),
                    in_specs=[pl.BlockSpec((SC_WINDOW, SC_COLS), index_map=lambda i, c=c: (i, c)),
                              pl.BlockSpec((1, SC_WINDOW), index_map=lambda i, k=k: (k, i))],
                    out_specs=[],
                    core_axis_name=("core", "subcore"),
                    dimension_semantics=(pltpu.PARALLEL,),
                )(x_hbm, i_hbm)

    return scatter(x, dest_t)


def _expert_body(ps_ref, cnt_ref, nu_ref, xs_hbm, wg_ref, wu_ref, wd_ref, ys_hbm,
                 xbuf, ybuf, insem, outsem, wgb_ref, wub_ref, wdb_ref):
    e = pl.program_id(0)
    bm = EXPERT_BM
    n_used = nu_ref[0]
    count = cnt_ref[e]
    n_blk = (count + bm - 1) // bm
    first = ps_ref[e] // bm

    def rows_of(g):
        return pl.ds(g * bm if isinstance(g, int) else pl.multiple_of(g * bm, bm), bm)

    def fetch(g):
        return pltpu.make_async_copy(xs_hbm.at[rows_of(g), :], xbuf.at[g % 2], insem.at[g % 2])

    def writeback(g):
        return pltpu.make_async_copy(ybuf.at[g % 2], ys_hbm.at[rows_of(g), :], outsem.at[g % 2])

    @pl.when(e == 0)
    def _():
        fetch(0).start()

    @pl.when(n_blk > 0)
    def _():
        wgb_ref[...] = wg_ref[0].astype(BF16)
        wub_ref[...] = wu_ref[0].astype(BF16)
        wdb_ref[...] = wd_ref[0].astype(BF16)

    def block(j, carry):
        g = first + j
        fetch(g).wait()
        pl.when(g + 1 < n_used)(fetch(g + 1).start)
        words = xbuf[g % 2]
        row = lax.broadcasted_iota(I32, words.shape, 0)
        lo, hi = _unpack_rows(jnp.where(row < count - j * bm, words, jnp.uint32(0)))
        lo, hi = lo.astype(BF16), hi.astype(BF16)
        gate = _dot(lo, wgb_ref[:HALF, :]) + _dot(hi, wgb_ref[HALF:, :])
        up = _dot(lo, wub_ref[:HALF, :]) + _dot(hi, wub_ref[HALF:, :])
        y = _pack_rows(_dot((_silu(gate) * up).astype(BF16), wdb_ref[...]))
        pl.when(g >= 2)(writeback(g - 2).wait)
        ybuf[g % 2] = y
        writeback(g).start()
        return carry

    lax.fori_loop(0, n_blk, block, 0)

    @pl.when(e == pl.num_programs(0) - 1)
    def _():
        pl.when(n_used >= 2)(writeback(n_used - 2).wait)
        writeback(n_used - 1).wait()


def _experts(pstarts, counts, n_used, xs, w_gate_e, w_up_e, w_down_e):
    n_slots = xs.shape[0]
    bm = EXPERT_BM
    wspec = lambda shape: pl.BlockSpec((1,) + shape, lambda e, ps, cnt, nu: (e, 0, 0))
    return pl.pallas_call(
        _expert_body,
        grid_spec=pltpu.PrefetchScalarGridSpec(
            num_scalar_prefetch=3,
            grid=(N_EXPERTS,),
            in_specs=[
                pl.BlockSpec(memory_space=pl.ANY),
                wspec((D_MODEL, EXPERT_HIDDEN)), wspec((D_MODEL, EXPERT_HIDDEN)), wspec((EXPERT_HIDDEN, D_MODEL)),
            ],
            out_specs=pl.BlockSpec(memory_space=pl.ANY),
            scratch_shapes=[pltpu.VMEM((2, bm, HALF), U32), pltpu.VMEM((2, bm, HALF), U32),
                            pltpu.SemaphoreType.DMA((2,)), pltpu.SemaphoreType.DMA((2,)),
                            pltpu.VMEM((D_MODEL, EXPERT_HIDDEN), BF16),
                            pltpu.VMEM((D_MODEL, EXPERT_HIDDEN), BF16),
                            pltpu.VMEM((EXPERT_HIDDEN, D_MODEL), BF16)],
        ),
        out_shape=jax.ShapeDtypeStruct((n_slots, HALF), U32),
        compiler_params=pltpu.CompilerParams(
            dimension_semantics=("arbitrary",), vmem_limit_bytes=VMEM_LIMIT),
        name="experts",
    )(pstarts, counts, n_used, xs, w_gate_e, w_up_e, w_down_e)


def _sc_gather_rows(table, idx_row):
    m = idx_row.shape[1]
    width = table.shape[1]

    @pl.kernel(out_type=jax.ShapeDtypeStruct((m, width), table.dtype), mesh=_sc_mesh())
    def gather(x_hbm, i_hbm, o_hbm):
        for c in range(width // SC_COLS):
            def body(i_vmem, o_vmem, c=c):
                pltpu.sync_copy(x_hbm.at[i_vmem.at[0], pl.ds(c * SC_COLS, SC_COLS)], o_vmem)

            pltpu.emit_pipeline(
                body,
                grid=(m // SC_WINDOW,),
                in_specs=[pl.BlockSpec((1, SC_WINDOW), index_map=lambda i: (0, i))],
                out_specs=[pl.BlockSpec((SC_WINDOW, SC_COLS), index_map=lambda i, c=c: (i, c))],
                core_axis_name=("core", "subcore"),
                dimension_semantics=(pltpu.PARALLEL,),
            )(i_hbm, o_hbm)

    return gather(table, idx_row)


COMBINE_TC = 256


def _combine_body(w_ref, h1_ref, *rest):
    rows = rest[:TOP_K]
    wgs_ref, wus_ref, wds_ref, g2_ref, b2_ref, o_ref = rest[TOP_K:]
    h1 = h1_ref[...]
    hb = h1.astype(BF16)
    shared = _dot((_silu(_dot(hb, wgs_ref[...])) * _dot(hb, wus_ref[...])).astype(BF16), wds_ref[...])
    w = w_ref[...]
    acc_lo = jnp.zeros((h1.shape[0], HALF), F32)
    acc_hi = jnp.zeros((h1.shape[0], HALF), F32)
    for k in range(TOP_K):
        lo, hi = _unpack_rows(rows[k][...])
        acc_lo = acc_lo + w[:, k:k + 1] * lo
        acc_hi = acc_hi + w[:, k:k + 1] * hi
    routed = jnp.concatenate([acc_lo, acc_hi], axis=1)
    o_ref[...] = _layer_norm(DN_ALPHA * h1 + routed + shared, g2_ref[...], b2_ref[...])


def _combine(w_tok, h1, gathered, wgs, wus, wds, g2, b2):
    t = h1.shape[0]
    tc = COMBINE_TC
    n_tiles = t // tc
    full = lambda a: pl.BlockSpec(a.shape, lambda i: (0,) * a.ndim)
    row_specs = [pl.BlockSpec((tc, HALF), lambda i, k=k: (k * n_tiles + i, 0)) for k in range(TOP_K)]
    return pl.pallas_call(
        _combine_body,
        grid=(n_tiles,),
        in_specs=[pl.BlockSpec((tc, TOP_K), lambda i: (i, 0)),
                  pl.BlockSpec((tc, D_MODEL), lambda i: (i, 0))] + row_specs
        + [full(wgs), full(wus), full(wds), full(g2), full(b2)],
        out_specs=pl.BlockSpec((tc, D_MODEL), lambda i: (i, 0)),
        out_shape=jax.ShapeDtypeStruct((t, D_MODEL), F32),
        compiler_params=pltpu.CompilerParams(
            dimension_semantics=("parallel",), vmem_limit_bytes=VMEM_LIMIT),
        name="combine_ln2",
    )(w_tok, h1, *([gathered] * TOP_K), wgs, wus, wds, g2, b2)


def _split_w_in(w):
    w = w.astype(BF16)
    sb = w[:, :SB_COLS]
    dq, dk, dv = (w[:, SB_COLS + j * DIL_WIDTH:SB_COLS + (j + 1) * DIL_WIDTH] for j in range(3))
    grp = lambda m, g: m[:, g * DIL_GROUP_WIDTH:(g + 1) * DIL_GROUP_WIDTH]
    dl = jnp.concatenate([grp(m, g) for g in range(N_DIL) for m in (dq, dk, dv)], axis=1)
    gates = w[:, SB_COLS + 3 * DIL_WIDTH:]
    return gates, sb, dl


def _moe(h1, h1_packed, scores_t, router_bias, w_gate_e, w_up_e, w_down_e, w_gate_s, w_up_s, w_down_s, g2, b2):
    t = h1.shape[0]
    idx_t, w_t, rank_t, counts = _route(scores_t, router_bias[:, None])
    counts = counts[:, 0].astype(I32)
    padded = (counts + EXPERT_BM - 1) // EXPERT_BM * EXPERT_BM
    pends = jnp.cumsum(padded)
    pstarts = (pends - padded).astype(I32)
    n_blocks = (t * TOP_K) // EXPERT_BM + N_EXPERTS
    n_used = (pends[-1:] // EXPERT_BM).astype(I32)
    dest_t = _dest(idx_t, rank_t, pstarts)
    xs = _sc_scatter_rows(h1_packed, dest_t, n_blocks * EXPERT_BM)
    ys = _experts(pstarts, counts, n_used, xs, w_gate_e, w_up_e, w_down_e)
    gathered = _sc_gather_rows(ys, dest_t.reshape(1, TOP_K * t))
    return _combine(w_t.T, h1, gathered,
                    w_gate_s.astype(BF16), w_up_s.astype(BF16), w_down_s.astype(BF16),
                    g2[None], b2[None])


def kernel(x, ln_in_g, ln_in_b, rel_bias, w_in, b_gate, w_br_sb, w_br_dil, w_out, ln1_g, ln1_b,
           w_router, router_bias, w_gate_e, w_up_e, w_down_e, w_gate_s, w_up_s, w_down_s,
           ln2_g, ln2_b):
    b, s, d = x.shape
    t = b * s
    n_layers = w_in.shape[0]
    h = x.reshape(t, d)
    ln_g, ln_b = ln_in_g[None], ln_in_b[None]
    bias_tabs = [_dil_bias_table(rel_bias, g, window, dil) for g, (window, dil) in enumerate(DIL_PATTERNS)]
    for l in range(n_layers):
        assert l == 0, "single-layer block"
        w_gate, w_sb, w_dl = _split_w_in(w_in[l])
        h, gates, sb, *dl_qkv = _inproj(h, ln_g, ln_b, w_gate, w_sb, w_dl, b)
        o_sb = _sb_attention(sb.reshape(b, s, SB_COLS)).reshape(t, SB_WIDTH)
        dl = [_dilated_group(dl_qkv[g], bias_tabs[g], g) for g in range(N_DIL)]
        wr = w_router[l].T
        wr_hi = wr.astype(BF16)
        wr_lo = (wr - wr_hi.astype(F32)).astype(BF16)
        h1, h1_packed, scores = _merge(gates, o_sb, dl, h, b_gate[l][None], w_br_sb[l].astype(BF16),
                                       w_br_dil[l].astype(BF16), w_out[l].astype(BF16),
                                       ln1_g[l][None], ln1_b[l][None], wr_hi, wr_lo)
        h = _moe(h1, h1_packed, scores, router_bias[l], w_gate_e[l], w_up_e[l], w_down_e[l],
                 w_gate_s[l], w_up_s[l], w_down_s[l], ln2_g[l], ln2_b[l])
    return h.reshape(b, s, d)
```

```python
import functools
import math

import numpy as np
import jax
import jax.numpy as jnp
from jax import lax
from jax.experimental import pallas as pl
from jax.experimental.pallas import tpu as pltpu
from jax.experimental.pallas import tpu_sc as plsc

F32 = jnp.float32
BF16 = jnp.bfloat16
I32 = jnp.int32
U32 = jnp.uint32

D_MODEL = 1024
HEAD_DIM = 64
SB_HEADS = 8
DIL_PATTERNS = ((128, 1), (512, 4), (2048, 16))
DIL_HEADS_PER_GROUP = 4
DIL_WINDOW_KEYS = 128
REL_BUCKETS = 32
REL_MAX_DISTANCE = 2048
N_EXPERTS = 256
TOP_K = 8
N_GROUPS = 8
GROUP_SIZE = N_EXPERTS // N_GROUPS
TOPK_GROUPS = 4
EXPERT_HIDDEN = 256
ROUTED_SCALE = 2.5
LN_EPS = 1e-5
DEPTH = 1
DN_ALPHA = (2 * DEPTH) ** 0.25

SB_WIDTH = SB_HEADS * HEAD_DIM
DIL_GROUP_WIDTH = DIL_HEADS_PER_GROUP * HEAD_DIM
DIL_WIDTH = DIL_GROUP_WIDTH * len(DIL_PATTERNS)

GATE_COLS = 2 * D_MODEL
SB_COLS = 3 * SB_WIDTH
DIL_GROUP_COLS = 3 * DIL_GROUP_WIDTH
N_DIL = len(DIL_PATTERNS)

LANES = 128
SUBLANES = 8
EXPERT_BM = 256
MASK_NEG = -1e30
SB_SKIP_LOG = -104.0

VMEM_LIMIT = 48 * 1024 * 1024


def _layer_norm(x, g, b):
    mu = jnp.mean(x, axis=-1, keepdims=True)
    xc = x - mu
    var = jnp.mean(xc * xc, axis=-1, keepdims=True)
    return xc * lax.rsqrt(var + LN_EPS) * g + b


def _dot(a, b):
    return jnp.dot(a, b, preferred_element_type=F32)


def _dot_nt(a, b):
    return lax.dot_general(a, b, (((1,), (1,)), ((), ())), preferred_element_type=F32)


def _silu(x):
    return x * jax.nn.sigmoid(x)


INPROJ_TM = 512


def _inproj_body(x_ref, g_ref, b_ref, wg_ref, wsb_ref, wdl_ref, h_ref, gate_ref, sb_ref, *rest):
    dl_refs, slab_ref = rest[:N_DIL], rest[N_DIL]
    tm = x_ref.shape[0]
    h = _layer_norm(x_ref[...], g_ref[...], b_ref[...])
    h_ref[...] = h
    hb = h.astype(BF16)
    gate_ref[...] = _dot(hb, wg_ref[...]).astype(BF16)
    sb_ref[...] = _dot(hb, wsb_ref[...]).astype(BF16)
    dl = _dot(hb, wdl_ref[...])
    slabs_per_group = DIL_GROUP_COLS // LANES
    for g, (_, dilation) in enumerate(DIL_PATTERNS):
        cols = dl[:, g * DIL_GROUP_COLS:(g + 1) * DIL_GROUP_COLS]
        if dilation == 1:
            dl_refs[g][0, 0] = cols.astype(BF16)
            continue
        for s in range(slabs_per_group):
            slab_ref[s] = cols[:, s * LANES:(s + 1) * LANES]
        for r in range(dilation):
            for s in range(slabs_per_group):
                rows = slab_ref[s, pl.ds(r, tm // dilation, stride=dilation), :]
                dl_refs[g][0, r, :, s * LANES:(s + 1) * LANES] = rows.astype(BF16)


def _inproj(x2, g, b, w_gate, w_sb, w_dl, batch):
    t = x2.shape[0]
    tm = INPROJ_TM
    s = t // batch
    tiles_per_seq = s // tm
    const = lambda a: pl.BlockSpec(a.shape, lambda i: (0,) * a.ndim, pipeline_mode=pl.Buffered(1))
    row = lambda w: pl.BlockSpec((tm, w), lambda i: (i, 0))
    dl_specs, dl_shapes = [], []
    for _, dilation in DIL_PATTERNS:
        dl_specs.append(pl.BlockSpec((1, dilation, tm // dilation, DIL_GROUP_COLS),
                                     lambda i: (i // tiles_per_seq, 0, i % tiles_per_seq, 0)))
        dl_shapes.append(jax.ShapeDtypeStruct((batch, dilation, s // dilation, DIL_GROUP_COLS), BF16))
    return pl.pallas_call(
        _inproj_body,
        grid=(t // tm,),
        in_specs=[row(D_MODEL), const(g), const(b), const(w_gate), const(w_sb), const(w_dl)],
        out_specs=[row(D_MODEL), row(GATE_COLS), row(SB_COLS)] + dl_specs,
        out_shape=[jax.ShapeDtypeStruct((t, D_MODEL), F32),
                   jax.ShapeDtypeStruct((t, GATE_COLS), BF16),
                   jax.ShapeDtypeStruct((t, SB_COLS), BF16)] + dl_shapes,
        scratch_shapes=[pltpu.VMEM((DIL_GROUP_COLS // LANES, tm, LANES), F32)],
        compiler_params=pltpu.CompilerParams(
            dimension_semantics=("parallel",), vmem_limit_bytes=VMEM_LIMIT),
        name="ln_inproj",
    )(x2, g, b, w_gate, w_sb, w_dl)


SB_TQ = 256
SB_TK = 128


def _sb_body(q_ref, k_ref, v_ref, o_ref, *, tq, scale):
    i = pl.program_id(2)
    tk = SB_TK
    q2 = q_ref[0] * scale
    lane = lax.broadcasted_iota(I32, (tq, LANES), 1)
    zero = jnp.zeros_like(q2)
    q_a = jnp.where(lane < HEAD_DIM, q2, zero)
    q_b = jnp.where(lane >= HEAD_DIM, q2, zero)

    def later_matrix(n):
        r = lax.broadcasted_iota(I32, (n, n), 0)
        c = lax.broadcasted_iota(I32, (n, n), 1)
        return (r > c).astype(BF16)

    def one_head(qm, carry, acc, kb, vb, later, before):
        z = _dot_nt(qm, kb)
        softplus = jnp.maximum(z, 0.0) + jnp.log(1.0 + jnp.exp(-jnp.abs(z)))
        log_keep = -softplus
        if before is not None:
            log_keep = jnp.where(before, log_keep, 0.0)
        lk_hi = log_keep.astype(BF16)
        lk_lo = (log_keep - lk_hi.astype(F32)).astype(BF16)
        between = _dot(lk_hi, later) + _dot(lk_lo, later)
        w = jnp.exp((z - softplus) + between + carry)
        if before is not None:
            w = jnp.where(before, w, 0.0)
        acc = acc + _dot(w.astype(BF16), vb)
        carry = carry + jnp.sum(log_keep, axis=-1, keepdims=True)
        return carry, acc

    row = lax.broadcasted_iota(I32, (tq, tq), 0)
    col = lax.broadcasted_iota(I32, (tq, tq), 1)
    tri = col < row
    d0 = pl.multiple_of(i * tq, tq)
    kd = k_ref[0, pl.ds(d0, tq), :]
    vd = v_ref[0, pl.ds(d0, tq), :]
    c0 = jnp.zeros((tq, 1), F32)
    a0 = jnp.zeros((tq, LANES), F32)
    later_d = later_matrix(tq)
    c_a, a_a = one_head(q_a, c0, a0, kd, vd, later_d, tri)
    c_b, a_b = one_head(q_b, c0, a0, kd, vd, later_d, tri)

    later_k = later_matrix(tk)

    def cond(st):
        return jnp.logical_and(st[0] >= 0, st[1] > 0)

    def body(st):
        j, _, c_a, c_b, a_a, a_b = st
        ks = pl.multiple_of(j * tk, tk)
        kb = k_ref[0, pl.ds(ks, tk), :]
        vb = v_ref[0, pl.ds(ks, tk), :]
        c_a, a_a = one_head(q_a, c_a, a_a, kb, vb, later_k, None)
        c_b, a_b = one_head(q_b, c_b, a_b, kb, vb, later_k, None)
        go = (jnp.max(jnp.maximum(c_a, c_b)) > SB_SKIP_LOG).astype(I32)
        return j - 1, go, c_a, c_b, a_a, a_b

    st = lax.while_loop(cond, body, (i * (tq // tk) - 1, jnp.int32(1), c_a, c_b, a_a, a_b))
    o_ref[0] = jnp.where(lane < HEAD_DIM, st[4], st[5]).astype(BF16)


def _sb_attention(proj3):
    b, s, _ = proj3.shape
    tq = SB_TQ
    qb, kb, vb = 0, SB_WIDTH // LANES, 2 * SB_WIDTH // LANES
    return pl.pallas_call(
        functools.partial(_sb_body, tq=tq, scale=1.0 / math.sqrt(HEAD_DIM)),
        grid=(b, SB_WIDTH // LANES, s // tq),
        in_specs=[
            pl.BlockSpec((1, tq, LANES), lambda bi, hp, i: (bi, i, qb + hp)),
            pl.BlockSpec((1, s, LANES), lambda bi, hp, i: (bi, 0, kb + hp)),
            pl.BlockSpec((1, s, LANES), lambda bi, hp, i: (bi, 0, vb + hp)),
        ],
        out_specs=pl.BlockSpec((1, tq, LANES), lambda bi, hp, i: (bi, i, hp)),
        out_shape=jax.ShapeDtypeStruct((b, s, SB_WIDTH), BF16),
        compiler_params=pltpu.CompilerParams(
            dimension_semantics=("parallel", "parallel", "arbitrary"), vmem_limit_bytes=VMEM_LIMIT),
        name="sb_attn",
    )(proj3, proj3, proj3)


DIL_TQ = 256
DIL_KP = DIL_WINDOW_KEYS


def _rel_bucket_idx(dist):
    max_exact = REL_BUCKETS // 2
    d = jnp.maximum(dist, 1).astype(F32)
    large = max_exact + (jnp.log(d / max_exact) / math.log(REL_MAX_DISTANCE / max_exact)
                         * (REL_BUCKETS - max_exact)).astype(I32)
    large = jnp.minimum(large, REL_BUCKETS - 1)
    return jnp.where(dist < max_exact, dist, large)


def _dil_bias_table(rel_bias, g, window, dilation):
    del window
    a = np.arange(DIL_TQ)[:, None]
    c = np.arange(DIL_KP + DIL_TQ)[None, :]
    step = a + DIL_KP - c
    valid = (step >= 0) & (step <= DIL_WINDOW_KEYS)
    bucket = _rel_bucket_idx(jnp.asarray(np.clip(step, 0, DIL_WINDOW_KEYS) * dilation, I32))
    heads = slice(g * DIL_HEADS_PER_GROUP, (g + 1) * DIL_HEADS_PER_GROUP)
    rb = rel_bias[:, heads].astype(F32)
    tab = jnp.zeros((DIL_HEADS_PER_GROUP,) + step.shape, F32)
    for bkt in range(REL_BUCKETS):
        tab = jnp.where(bucket[None] == bkt, rb[bkt][:, None, None], tab)
    return jnp.where(valid[None], tab, MASK_NEG)


def _dil_body(q_ref, kp_ref, kc_ref, vp_ref, vc_ref, bias_ref, o_ref, lse_ref, *, tq, scale):
    i = pl.program_id(2)
    q = q_ref[0]
    k2 = jnp.concatenate([kp_ref[0], kc_ref[0]], axis=0)
    v2 = jnp.concatenate([vp_ref[0], vc_ref[0]], axis=0)
    width = DIL_GROUP_WIDTH
    lane = lax.broadcasted_iota(I32, (tq, width), 1)
    keycol = lax.broadcasted_iota(I32, (tq, DIL_KP + tq), 1)
    no_prev = jnp.logical_and(i == 0, keycol < DIL_KP)
    zero = jnp.zeros_like(q)
    out = jnp.zeros((tq, width), F32)
    lse_out = jnp.zeros((tq, width), F32)
    for h in range(DIL_HEADS_PER_GROUP):
        in_head = jnp.logical_and(lane >= h * HEAD_DIM, lane < (h + 1) * HEAD_DIM)
        qm = jnp.where(in_head, q, zero)
        s = _dot_nt(qm, k2) * scale + bias_ref[h]
        s = jnp.where(no_prev, MASK_NEG, s)
        m = jnp.max(s, axis=-1, keepdims=True)
        p = jnp.exp(s - m)
        l = jnp.sum(p, axis=-1, keepdims=True)
        pv = _dot(p.astype(BF16), v2)
        out = jnp.where(in_head, pv / l, out)
        lse_out = jnp.where(in_head, m + jnp.log(l), lse_out)
    o_ref[0] = out
    lse_ref[0] = lse_out


def _dilated_group(qkv, bias_tab, g):
    b, dilation, sd, _ = qkv.shape
    tq = DIL_TQ
    width = DIL_GROUP_WIDTH
    blk = (1, None, tq, width)
    prev = (1, None, DIL_KP, width)
    per = tq // DIL_KP
    prev_row = lambda i: jnp.maximum(i * per - 1, 0)
    o, lse = pl.pallas_call(
        functools.partial(_dil_body, tq=tq, scale=1.0 / math.sqrt(HEAD_DIM)),
        grid=(b, dilation, sd // tq),
        in_specs=[
            pl.BlockSpec(blk, lambda bi, r, i: (bi, r, i, 0)),
            pl.BlockSpec(prev, lambda bi, r, i: (bi, r, prev_row(i), 1)),
            pl.BlockSpec(blk, lambda bi, r, i: (bi, r, i, 1)),
            pl.BlockSpec(prev, lambda bi, r, i: (bi, r, prev_row(i), 2)),
            pl.BlockSpec(blk, lambda bi, r, i: (bi, r, i, 2)),
            pl.BlockSpec((DIL_HEADS_PER_GROUP, tq, DIL_KP + tq), lambda bi, r, i: (0, 0, 0)),
        ],
        out_specs=[
            pl.BlockSpec((1, tq, width), lambda bi, r, i: (bi, i, r)),
            pl.BlockSpec((1, tq, width), lambda bi, r, i: (bi, i, r)),
        ],
        out_shape=[
            jax.ShapeDtypeStruct((b, sd, dilation * width), F32),
            jax.ShapeDtypeStruct((b, sd, dilation * width), F32),
        ],
        compiler_params=pltpu.CompilerParams(
            dimension_semantics=("parallel", "parallel", "arbitrary"), vmem_limit_bytes=VMEM_LIMIT),
        name=f"dil_attn_g{g}",
    )(qkv, qkv, qkv, qkv, qkv, bias_tab)
    t = b * dilation * sd
    return o.reshape(t, width), lse.reshape(t, width)


MERGE_TM = 256


def _merge_body(gp_ref, osb_ref, o0_ref, o1_ref, o2_ref, l0_ref, l1_ref, l2_ref, h_ref, bg_ref,
                wsb_ref, wdl_ref, wout_ref, g1_ref, b1_ref, wrh_ref, wrl_ref, h1_ref, h1p_ref, sc_ref):
    l0, l1, l2 = l0_ref[...], l1_ref[...], l2_ref[...]
    m = jnp.maximum(jnp.maximum(l0, l1), l2)
    e0, e1, e2 = jnp.exp(l0 - m), jnp.exp(l1 - m), jnp.exp(l2 - m)
    o_dl = (e0 * o0_ref[...] + e1 * o1_ref[...] + e2 * o2_ref[...]) / (e0 + e1 + e2)
    br_sb = _dot(osb_ref[...], wsb_ref[...])
    br_dl = _dot(o_dl.astype(BF16), wdl_ref[...])
    gates = jax.nn.sigmoid(gp_ref[...].astype(F32) + bg_ref[...])
    merged = gates[:, :D_MODEL] * br_sb + gates[:, D_MODEL:] * br_dl
    mix = _dot(merged.astype(BF16), wout_ref[...])
    h1 = _layer_norm(DN_ALPHA * h_ref[...] + mix, g1_ref[...], b1_ref[...])
    h1_ref[...] = h1
    h1p_ref[...] = _pack_rows(h1)
    h_hi = h1.astype(BF16)
    h_lo = (h1 - h_hi.astype(F32)).astype(BF16)
    logits = _dot_nt(wrh_ref[...], h_hi) + _dot_nt(wrh_ref[...], h_lo) + _dot_nt(wrl_ref[...], h_hi)
    sc_ref[...] = jax.nn.sigmoid(logits)


def _merge(gates, o_sb, dl, h, b_gate, w_br_sb, w_br_dil, w_out, g1, b1, wr_hi_t, wr_lo_t):
    t = h.shape[0]
    tm = MERGE_TM
    row = lambda w: pl.BlockSpec((tm, w), lambda i: (i, 0))
    full = lambda a: pl.BlockSpec(a.shape, lambda i: (0,) * a.ndim)
    args = (gates, o_sb, dl[0][0], dl[1][0], dl[2][0], dl[0][1], dl[1][1], dl[2][1], h, b_gate,
            w_br_sb, w_br_dil, w_out, g1, b1, wr_hi_t, wr_lo_t)
    in_specs = [row(GATE_COLS), row(SB_WIDTH)] + [row(DIL_GROUP_WIDTH)] * 6 + [row(D_MODEL)] \
        + [full(a) for a in args[9:]]
    return pl.pallas_call(
        _merge_body,
        grid=(t // tm,),
        in_specs=in_specs,
        out_specs=[row(D_MODEL), row(D_MODEL // 2), pl.BlockSpec((N_EXPERTS, tm), lambda i: (0, i))],
        out_shape=[jax.ShapeDtypeStruct((t, D_MODEL), F32), jax.ShapeDtypeStruct((t, D_MODEL // 2), U32),
                   jax.ShapeDtypeStruct((N_EXPERTS, t), F32)],
        compiler_params=pltpu.CompilerParams(
            dimension_semantics=("parallel",), vmem_limit_bytes=VMEM_LIMIT),
        name="merge_ln1",
    )(*args)


ROUTE_TR = 256


def _route_body(sc_ref, rb_ref, idx_ref, w_ref, rank_ref, cnt_ref, carry_ref, *, tr):
    @pl.when(pl.program_id(0) == 0)
    def _():
        carry_ref[...] = jnp.zeros_like(carry_ref)

    scores = sc_ref[...]
    biased = scores + rb_ref[...]
    erow = lax.broadcasted_iota(I32, (N_EXPERTS, tr), 0)
    neg = -jnp.inf

    def first_argmax(v, rows):
        m = jnp.max(v, axis=0, keepdims=True)
        return m, jnp.min(jnp.where(v == m, rows, N_EXPERTS), axis=0, keepdims=True)

    gslice = lambda a, g: a[g * GROUP_SIZE:(g + 1) * GROUP_SIZE, :]
    gscore = []
    grow = lax.broadcasted_iota(I32, (GROUP_SIZE, tr), 0)
    for g in range(N_GROUPS):
        vg, rg = gslice(biased, g), grow + g * GROUP_SIZE
        m1, i1 = first_argmax(vg, rg)
        m2 = jnp.max(jnp.where(rg == i1, neg, vg), axis=0, keepdims=True)
        gscore.append(m1 + m2)
    pieces = []
    for g in range(N_GROUPS):
        beaten = jnp.zeros((1, tr), I32)
        for g2 in range(N_GROUPS):
            if g2 == g:
                continue
            wins = gscore[g2] > gscore[g]
            if g2 < g:
                wins = jnp.logical_or(wins, gscore[g2] == gscore[g])
            beaten = beaten + wins.astype(I32)
        pieces.append(jnp.where(beaten < TOPK_GROUPS, gslice(biased, g), neg))
    masked = jnp.concatenate(pieces, axis=0)
    sel = jnp.zeros((N_EXPERTS, tr), jnp.bool_)
    picks = []
    for _ in range(TOP_K):
        _, ik = first_argmax(masked, erow)
        hit = erow == ik
        sel = jnp.logical_or(sel, hit)
        masked = jnp.where(hit, neg, masked)
        picks.append(ik)
    denom = jnp.sum(jnp.where(sel, scores, 0.0), axis=0, keepdims=True)

    r = lax.broadcasted_iota(I32, (tr, tr), 0)
    c = lax.broadcasted_iota(I32, (tr, tr), 1)
    earlier = (r < c).astype(BF16)
    sel_f = sel.astype(F32)
    before = _dot(sel_f.astype(BF16), earlier) + carry_ref[:, 0:1]
    new_carry = carry_ref[...] + jnp.sum(sel_f, axis=1, keepdims=True)
    carry_ref[...] = new_carry
    cnt_ref[...] = new_carry

    krow = lax.broadcasted_iota(I32, (TOP_K, tr), 0)
    idx_o = jnp.zeros((TOP_K, tr), I32)
    rank_o = jnp.zeros((TOP_K, tr), I32)
    w_o = jnp.zeros((TOP_K, tr), F32)
    for k, ik in enumerate(picks):
        hit = erow == ik
        wk = jnp.sum(jnp.where(hit, scores, 0.0), axis=0, keepdims=True) / denom * ROUTED_SCALE
        rk = jnp.sum(jnp.where(hit, before, 0.0), axis=0, keepdims=True).astype(I32)
        idx_o = jnp.where(krow == k, ik, idx_o)
        rank_o = jnp.where(krow == k, rk, rank_o)
        w_o = jnp.where(krow == k, wk, w_o)
    idx_ref[...] = idx_o
    rank_ref[...] = rank_o
    w_ref[...] = w_o


def _route(scores_t, router_bias_col):
    t = scores_t.shape[1]
    tr = ROUTE_TR
    col = lambda n: pl.BlockSpec((n, tr), lambda i: (0, i))
    return pl.pallas_call(
        functools.partial(_route_body, tr=tr),
        grid=(t // tr,),
        in_specs=[col(N_EXPERTS), pl.BlockSpec((N_EXPERTS, 1), lambda i: (0, 0))],
        out_specs=[col(TOP_K), col(TOP_K), col(TOP_K), pl.BlockSpec((N_EXPERTS, LANES), lambda i: (0, 0))],
        out_shape=[jax.ShapeDtypeStruct((TOP_K, t), I32), jax.ShapeDtypeStruct((TOP_K, t), F32),
                   jax.ShapeDtypeStruct((TOP_K, t), I32), jax.ShapeDtypeStruct((N_EXPERTS, LANES), F32)],
        scratch_shapes=[pltpu.VMEM((N_EXPERTS, LANES), F32)],
        compiler_params=pltpu.CompilerParams(
            dimension_semantics=("arbitrary",), vmem_limit_bytes=VMEM_LIMIT),
        name="route",
    )(scores_t, router_bias_col)


def _dest_body(idx_ref, rank_ref, ps_ref, dest_ref, *, tr):
    idx = idx_ref[...]
    erow = lax.broadcasted_iota(I32, (N_EXPERTS, tr), 0)
    starts = jnp.concatenate([ps_ref[...]] * (tr // LANES), axis=1)
    krow = lax.broadcasted_iota(I32, (TOP_K, tr), 0)
    out = rank_ref[...]
    for k in range(TOP_K):
        start_k = jnp.sum(jnp.where(erow == idx[k:k + 1, :], starts, 0), axis=0, keepdims=True)
        out = out + jnp.where(krow == k, start_k, 0)
    dest_ref[...] = out


def _dest(idx_t, rank_t, pstarts):
    t = idx_t.shape[1]
    tr = ROUTE_TR
    col = pl.BlockSpec((TOP_K, tr), lambda i: (0, i))
    starts = jnp.broadcast_to(pstarts[:, None], (N_EXPERTS, LANES))
    return pl.pallas_call(
        functools.partial(_dest_body, tr=tr),
        grid=(t // tr,),
        in_specs=[col, col, pl.BlockSpec((N_EXPERTS, LANES), lambda i: (0, 0))],
        out_specs=col,
        out_shape=jax.ShapeDtypeStruct((TOP_K, t), I32),
        compiler_params=pltpu.CompilerParams(
            dimension_semantics=("parallel",), vmem_limit_bytes=VMEM_LIMIT),
        name="dest",
    )(idx_t, rank_t, starts)


SC_WINDOW = 128
SC_COLS = 256
HALF = D_MODEL // 2


def _pack_rows(x):
    bits = lambda v: lax.bitcast_convert_type(v.astype(BF16).astype(F32), U32)
    return bits(x[:, HALF:]) | (bits(x[:, :HALF]) >> 16)


def _unpack_rows(w):
    lo = lax.bitcast_convert_type(w << 16, F32)
    hi = lax.bitcast_convert_type(w & jnp.uint32(0xFFFF0000), F32)
    return lo, hi


def _sc_mesh():
    return plsc.VectorSubcoreMesh(core_axis_name="core", subcore_axis_name="subcore")


def _sc_scatter_rows(x, dest_t, n_slots):
    t, width = x.shape

    @pl.kernel(out_type=jax.ShapeDtypeStruct((n_slots, width), x.dtype), mesh=_sc_mesh())
    def scatter(x_hbm, i_hbm, o_hbm):
        for k in range(TOP_K):
            for c in range(width // SC_COLS):
                def body(x_vmem, i_vmem, c=c):
                    pltpu.sync_copy(x_vmem, o_hbm.at[i_vmem.at[0], pl.ds(c * SC_COLS, SC_COLS)])

                pltpu.emit_pipeline(
                    body,
                    grid=(t // SC_WINDOW,),
                    in_specs=[pl.BlockSpec((SC_WINDOW, SC_COLS), index_map=lambda i, c=c: (i, c)),
                              pl.BlockSpec((1, SC_WINDOW), index_map=lambda i, k=k: (k, i))],
                    out_specs=[],
                    core_axis_name=("core", "subcore"),
                    dimension_semantics=(pltpu.PARALLEL,),
                )(x_hbm, i_hbm)

    return scatter(x, dest_t)


def _expert_body(ps_ref, cnt_ref, nu_ref, xs_hbm, wg_ref, wu_ref, wd_ref, ys_hbm,
                 xbuf, ybuf, insem, outsem, wgb_ref, wub_ref, wdb_ref):
    e = pl.program_id(0)
    bm = EXPERT_BM
    n_used = nu_ref[0]
    count = cnt_ref[e]
    n_blk = (count + bm - 1) // bm
    first = ps_ref[e] // bm

    def rows_of(g):
        return pl.ds(g * bm if isinstance(g, int) else pl.multiple_of(g * bm, bm), bm)

    def fetch(g):
        return pltpu.make_async_copy(xs_hbm.at[rows_of(g), :], xbuf.at[g % 2], insem.at[g % 2])

    def writeback(g):
        return pltpu.make_async_copy(ybuf.at[g % 2], ys_hbm.at[rows_of(g), :], outsem.at[g % 2])

    @pl.when(e == 0)
    def _():
        fetch(0).start()

    @pl.when(n_blk > 0)
    def _():
        wgb_ref[...] = wg_ref[0].astype(BF16)
        wub_ref[...] = wu_ref[0].astype(BF16)
        wdb_ref[...] = wd_ref[0].astype(BF16)

    def block(j, carry):
        g = first + j
        fetch(g).wait()
        pl.when(g + 1 < n_used)(fetch(g + 1).start)
        words = xbuf[g % 2]
        row = lax.broadcasted_iota(I32, words.shape, 0)
        lo, hi = _unpack_rows(jnp.where(row < count - j * bm, words, jnp.uint32(0)))
        lo, hi = lo.astype(BF16), hi.astype(BF16)
        gate = _dot(lo, wgb_ref[:HALF, :]) + _dot(hi, wgb_ref[HALF:, :])
        up = _dot(lo, wub_ref[:HALF, :]) + _dot(hi, wub_ref[HALF:, :])
        y = _pack_rows(_dot((_silu(gate) * up).astype(BF16), wdb_ref[...]))
        pl.when(g >= 2)(writeback(g - 2).wait)
        ybuf[g % 2] = y
        writeback(g).start()
        return carry

    lax.fori_loop(0, n_blk, block, 0)

    @pl.when(e == pl.num_programs(0) - 1)
    def _():
        pl.when(n_used >= 2)(writeback(n_used - 2).wait)
        writeback(n_used - 1).wait()


def _experts(pstarts, counts, n_used, xs, w_gate_e, w_up_e, w_down_e):
    n_slots = xs.shape[0]
    bm = EXPERT_BM
    wspec = lambda shape: pl.BlockSpec((1,) + shape, lambda e, ps, cnt, nu: (e, 0, 0))
    return pl.pallas_call(
        _expert_body,
        grid_spec=pltpu.PrefetchScalarGridSpec(
            num_scalar_prefetch=3,
            grid=(N_EXPERTS,),
            in_specs=[
                pl.BlockSpec(memory_space=pl.ANY),
                wspec((D_MODEL, EXPERT_HIDDEN)), wspec((D_MODEL, EXPERT_HIDDEN)), wspec((EXPERT_HIDDEN, D_MODEL)),
            ],
            out_specs=pl.BlockSpec(memory_space=pl.ANY),
            scratch_shapes=[pltpu.VMEM((2, bm, HALF), U32), pltpu.VMEM((2, bm, HALF), U32),
                            pltpu.SemaphoreType.DMA((2,)), pltpu.SemaphoreType.DMA((2,)),
                            pltpu.VMEM((D_MODEL, EXPERT_HIDDEN), BF16),
                            pltpu.VMEM((D_MODEL, EXPERT_HIDDEN), BF16),
                            pltpu.VMEM((EXPERT_HIDDEN, D_MODEL), BF16)],
        ),
        out_shape=jax.ShapeDtypeStruct((n_slots, HALF), U32),
        compiler_params=pltpu.CompilerParams(
            dimension_semantics=("arbitrary",), vmem_limit_bytes=VMEM_LIMIT),
        name="experts",
    )(pstarts, counts, n_used, xs, w_gate_e, w_up_e, w_down_e)


def _sc_gather_rows(table, idx_row):
    m = idx_row.shape[1]
    width = table.shape[1]

    @pl.kernel(out_type=jax.ShapeDtypeStruct((m, width), table.dtype), mesh=_sc_mesh())
    def gather(x_hbm, i_hbm, o_hbm):
        for c in range(width // SC_COLS):
            def body(i_vmem, o_vmem, c=c):
                pltpu.sync_copy(x_hbm.at[i_vmem.at[0], pl.ds(c * SC_COLS, SC_COLS)], o_vmem)

            pltpu.emit_pipeline(
                body,
                grid=(m // SC_WINDOW,),
                in_specs=[pl.BlockSpec((1, SC_WINDOW), index_map=lambda i: (0, i))],
                out_specs=[pl.BlockSpec((SC_WINDOW, SC_COLS), index_map=lambda i, c=c: (i, c))],
                core_axis_name=("core", "subcore"),
                dimension_semantics=(pltpu.PARALLEL,),
            )(i_hbm, o_hbm)

    return gather(table, idx_row)


COMBINE_TC = 256


def _combine_body(w_ref, h1_ref, *rest):
    rows = rest[:TOP_K]
    wgs_ref, wus_ref, wds_ref, g2_ref, b2_ref, o_ref = rest[TOP_K:]
    h1 = h1_ref[...]
    hb = h1.astype(BF16)
    shared = _dot((_silu(_dot(hb, wgs_ref[...])) * _dot(hb, wus_ref[...])).astype(BF16), wds_ref[...])
    w = w_ref[...]
    acc_lo = jnp.zeros((h1.shape[0], HALF), F32)
    acc_hi = jnp.zeros((h1.shape[0], HALF), F32)
    for k in range(TOP_K):
        lo, hi = _unpack_rows(rows[k][...])
        acc_lo = acc_lo + w[:, k:k + 1] * lo
        acc_hi = acc_hi + w[:, k:k + 1] * hi
    routed = jnp.concatenate([acc_lo, acc_hi], axis=1)
    o_ref[...] = _layer_norm(DN_ALPHA * h1 + routed + shared, g2_ref[...], b2_ref[...])


def _combine(w_tok, h1, gathered, wgs, wus, wds, g2, b2):
    t = h1.shape[0]
    tc = COMBINE_TC
    n_tiles = t // tc
    full = lambda a: pl.BlockSpec(a.shape, lambda i: (0,) * a.ndim)
    row_specs = [pl.BlockSpec((tc, HALF), lambda i, k=k: (k * n_tiles + i, 0)) for k in range(TOP_K)]
    return pl.pallas_call(
        _combine_body,
        grid=(n_tiles,),
        in_specs=[pl.BlockSpec((tc, TOP_K), lambda i: (i, 0)),
                  pl.BlockSpec((tc, D_MODEL), lambda i: (i, 0))] + row_specs
        + [full(wgs), full(wus), full(wds), full(g2), full(b2)],
        out_specs=pl.BlockSpec((tc, D_MODEL), lambda i: (i, 0)),
        out_shape=jax.ShapeDtypeStruct((t, D_MODEL), F32),
        compiler_params=pltpu.CompilerParams(
            dimension_semantics=("parallel",), vmem_limit_bytes=VMEM_LIMIT),
        name="combine_ln2",
    )(w_tok, h1, *([gathered] * TOP_K), wgs, wus, wds, g2, b2)


def _split_w_in(w):
    w = w.astype(BF16)
    sb = w[:, :SB_COLS]
    dq, dk, dv = (w[:, SB_COLS + j * DIL_WIDTH:SB_COLS + (j + 1) * DIL_WIDTH] for j in range(3))
    grp = lambda m, g: m[:, g * DIL_GROUP_WIDTH:(g + 1) * DIL_GROUP_WIDTH]
    dl = jnp.concatenate([grp(m, g) for g in range(N_DIL) for m in (dq, dk, dv)], axis=1)
    gates = w[:, SB_COLS + 3 * DIL_WIDTH:]
    return gates, sb, dl


def _moe(h1, h1_packed, scores_t, router_bias, w_gate_e, w_up_e, w_down_e, w_gate_s, w_up_s, w_down_s, g2, b2):
    t = h1.shape[0]
    idx_t, w_t, rank_t, counts = _route(scores_t, router_bias[:, None])
    counts = counts[:, 0].astype(I32)
    padded = (counts + EXPERT_BM - 1) // EXPERT_BM * EXPERT_BM
    pends = jnp.cumsum(padded)
    pstarts = (pends - padded).astype(I32)
    n_blocks = (t * TOP_K) // EXPERT_BM + N_EXPERTS
    n_used = (pends[-1:] // EXPERT_BM).astype(I32)
    dest_t = _dest(idx_t, rank_t, pstarts)
    xs = _sc_scatter_rows(h1_packed, dest_t, n_blocks * EXPERT_BM)
    ys = _experts(pstarts, counts, n_used, xs, w_gate_e, w_up_e, w_down_e)
    gathered = _sc_gather_rows(ys, dest_t.reshape(1, TOP_K * t))
    return _combine(w_t.T, h1, gathered,
                    w_gate_s.astype(BF16), w_up_s.astype(BF16), w_down_s.astype(BF16),
                    g2[None], b2[None])


def kernel(x, ln_in_g, ln_in_b, rel_bias, w_in, b_gate, w_br_sb, w_br_dil, w_out, ln1_g, ln1_b,
           w_router, router_bias, w_gate_e, w_up_e, w_down_e, w_gate_s, w_up_s, w_down_s,
           ln2_g, ln2_b):
    b, s, d = x.shape
    t = b * s
    n_layers = w_in.shape[0]
    h = x.reshape(t, d)
    ln_g, ln_b = ln_in_g[None], ln_in_b[None]
    bias_tabs = [_dil_bias_table(rel_bias, g, window, dil) for g, (window, dil) in enumerate(DIL_PATTERNS)]
    for l in range(n_layers):
        assert l == 0, "single-layer block"
        w_gate, w_sb, w_dl = _split_w_in(w_in[l])
        h, gates, sb, *dl_qkv = _inproj(h, ln_g, ln_b, w_gate, w_sb, w_dl, b)
        o_sb = _sb_attention(sb.reshape(b, s, SB_COLS)).reshape(t, SB_WIDTH)
        dl = [_dilated_group(dl_qkv[g], bias_tabs[g], g) for g in range(N_DIL)]
        wr = w_router[l].T
        wr_hi = wr.astype(BF16)
        wr_lo = (wr - wr_hi.astype(F32)).astype(BF16)
        h1, h1_packed, scores = _merge(gates, o_sb, dl, h, b_gate[l][None], w_br_sb[l].astype(BF16),
                                       w_br_dil[l].astype(BF16), w_out[l].astype(BF16),
                                       ln1_g[l][None], ln1_b[l][None], wr_hi, wr_lo)
        h = _moe(h1, h1_packed, scores, router_bias[l], w_gate_e[l], w_up_e[l], w_down_e[l],
                 w_gate_s[l], w_up_s[l], w_down_s[l], ln2_g[l], ln2_b[l])
    return h.reshape(b, s, d)
```

```python
import functools
import math

import numpy as np
import jax
import jax.numpy as jnp
from jax import lax
from jax.experimental import pallas as pl
from jax.experimental.pallas import tpu as pltpu
from jax.experimental.pallas import tpu_sc as plsc

F32 = jnp.float32
BF16 = jnp.bfloat16
I32 = jnp.int32
U32 = jnp.uint32

D_MODEL = 1024
HEAD_DIM = 64
SB_HEADS = 8
DIL_PATTERNS = ((128, 1), (512, 4), (2048, 16))
DIL_HEADS_PER_GROUP = 4
DIL_WINDOW_KEYS = 128
REL_BUCKETS = 32
REL_MAX_DISTANCE = 2048
N_EXPERTS = 256
TOP_K = 8
N_GROUPS = 8
GROUP_SIZE = N_EXPERTS // N_GROUPS
TOPK_GROUPS = 4
EXPERT_HIDDEN = 256
ROUTED_SCALE = 2.5
LN_EPS = 1e-5
DEPTH = 1
DN_ALPHA = (2 * DEPTH) ** 0.25

SB_WIDTH = SB_HEADS * HEAD_DIM
DIL_GROUP_WIDTH = DIL_HEADS_PER_GROUP * HEAD_DIM
DIL_WIDTH = DIL_GROUP_WIDTH * len(DIL_PATTERNS)

GATE_COLS = 2 * D_MODEL
SB_COLS = 3 * SB_WIDTH
DIL_GROUP_COLS = 3 * DIL_GROUP_WIDTH
N_DIL = len(DIL_PATTERNS)

LANES = 128
SUBLANES = 8
EXPERT_BM = 256
MASK_NEG = -1e30
SB_SKIP_LOG = -104.0

VMEM_LIMIT = 48 * 1024 * 1024


def _layer_norm(x, g, b):
    mu = jnp.mean(x, axis=-1, keepdims=True)
    xc = x - mu
    var = jnp.mean(xc * xc, axis=-1, keepdims=True)
    return xc * lax.rsqrt(var + LN_EPS) * g + b


def _dot(a, b):
    return jnp.dot(a, b, preferred_element_type=F32)


def _dot_nt(a, b):
    return lax.dot_general(a, b, (((1,), (1,)), ((), ())), preferred_element_type=F32)


def _silu(x):
    return x * jax.nn.sigmoid(x)


INPROJ_TM = 512


def _inproj_body(x_ref, g_ref, b_ref, wg_ref, wsb_ref, wdl_ref, h_ref, gate_ref, sb_ref, *rest):
    dl_refs, slab_ref = rest[:N_DIL], rest[N_DIL]
    tm = x_ref.shape[0]
    h = _layer_norm(x_ref[...], g_ref[...], b_ref[...])
    h_ref[...] = h
    hb = h.astype(BF16)
    gate_ref[...] = _dot(hb, wg_ref[...]).astype(BF16)
    sb_ref[...] = _dot(hb, wsb_ref[...]).astype(BF16)
    dl = _dot(hb, wdl_ref[...])
    slabs_per_group = DIL_GROUP_COLS // LANES
    for g, (_, dilation) in enumerate(DIL_PATTERNS):
        cols = dl[:, g * DIL_GROUP_COLS:(g + 1) * DIL_GROUP_COLS]
        if dilation == 1:
            dl_refs[g][0, 0] = cols.astype(BF16)
            continue
        for s in range(slabs_per_group):
            slab_ref[s] = cols[:, s * LANES:(s + 1) * LANES]
        for r in range(dilation):
            for s in range(slabs_per_group):
                rows = slab_ref[s, pl.ds(r, tm // dilation, stride=dilation), :]
                dl_refs[g][0, r, :, s * LANES:(s + 1) * LANES] = rows.astype(BF16)


def _inproj(x2, g, b, w_gate, w_sb, w_dl, batch):
    t = x2.shape[0]
    tm = INPROJ_TM
    s = t // batch
    tiles_per_seq = s // tm
    const = lambda a: pl.BlockSpec(a.shape, lambda i: (0,) * a.ndim, pipeline_mode=pl.Buffered(1))
    row = lambda w: pl.BlockSpec((tm, w), lambda i: (i, 0))
    dl_specs, dl_shapes = [], []
    for _, dilation in DIL_PATTERNS:
        dl_specs.append(pl.BlockSpec((1, dilation, tm // dilation, DIL_GROUP_COLS),
                                     lambda i: (i // tiles_per_seq, 0, i % tiles_per_seq, 0)))
        dl_shapes.append(jax.ShapeDtypeStruct((batch, dilation, s // dilation, DIL_GROUP_COLS), BF16))
    return pl.pallas_call(
        _inproj_body,
        grid=(t // tm,),
        in_specs=[row(D_MODEL), const(g), const(b), const(w_gate), const(w_sb), const(w_dl)],
        out_specs=[row(D_MODEL), row(GATE_COLS), row(SB_COLS)] + dl_specs,
        out_shape=[jax.ShapeDtypeStruct((t, D_MODEL), F32),
                   jax.ShapeDtypeStruct((t, GATE_COLS), BF16),
                   jax.ShapeDtypeStruct((t, SB_COLS), BF16)] + dl_shapes,
        scratch_shapes=[pltpu.VMEM((DIL_GROUP_COLS // LANES, tm, LANES), F32)],
        compiler_params=pltpu.CompilerParams(
            dimension_semantics=("parallel",), vmem_limit_bytes=VMEM_LIMIT),
        name="ln_inproj",
    )(x2, g, b, w_gate, w_sb, w_dl)


SB_TQ = 256


def _sb_body(q_ref, k_ref, v_ref, o_ref, *, tq, scale):
    i = pl.program_id(2)
    q2 = q_ref[0] * scale
    lane = lax.broadcasted_iota(I32, (tq, LANES), 1)
    zero = jnp.zeros_like(q2)
    q_a = jnp.where(lane < HEAD_DIM, q2, zero)
    q_b = jnp.where(lane >= HEAD_DIM, q2, zero)
    row = lax.broadcasted_iota(I32, (tq, tq), 0)
    col = lax.broadcasted_iota(I32, (tq, tq), 1)
    tri = col < row
    later = (row > col).astype(BF16)

    def one_head(qm, carry, acc, kb, vb, before):
        z = _dot_nt(qm, kb)
        softplus = jnp.maximum(z, 0.0) + jnp.log(1.0 + jnp.exp(-jnp.abs(z)))
        log_keep = jnp.where(before, -softplus, 0.0)
        lk_hi = log_keep.astype(BF16)
        lk_lo = (log_keep - lk_hi.astype(F32)).astype(BF16)
        between = _dot(lk_hi, later) + _dot(lk_lo, later)
        total = (z - softplus) + between + carry
        w = jnp.where(before, jnp.exp(total), 0.0)
        acc = acc + _dot(w.astype(BF16), vb)
        carry = carry + jnp.sum(log_keep, axis=-1, keepdims=True)
        return carry, acc

    def cond(st):
        return jnp.logical_and(st[0] >= 0, st[1] > 0)

    def body(st):
        j, _, c_a, c_b, a_a, a_b = st
        ks = pl.multiple_of(j * tq, tq)
        kb = k_ref[0, pl.ds(ks, tq), :]
        vb = v_ref[0, pl.ds(ks, tq), :]
        before = jnp.logical_or(tri, j < i)
        c_a, a_a = one_head(q_a, c_a, a_a, kb, vb, before)
        c_b, a_b = one_head(q_b, c_b, a_b, kb, vb, before)
        go = (jnp.max(jnp.maximum(c_a, c_b)) > SB_SKIP_LOG).astype(I32)
        return j - 1, go, c_a, c_b, a_a, a_b

    c0 = jnp.zeros((tq, 1), F32)
    a0 = jnp.zeros((tq, LANES), F32)
    st = lax.while_loop(cond, body, (i, jnp.int32(1), c0, c0, a0, a0))
    o_ref[0] = jnp.where(lane < HEAD_DIM, st[4], st[5]).astype(BF16)


def _sb_attention(proj3):
    b, s, _ = proj3.shape
    tq = SB_TQ
    qb, kb, vb = 0, SB_WIDTH // LANES, 2 * SB_WIDTH // LANES
    return pl.pallas_call(
        functools.partial(_sb_body, tq=tq, scale=1.0 / math.sqrt(HEAD_DIM)),
        grid=(b, SB_WIDTH // LANES, s // tq),
        in_specs=[
            pl.BlockSpec((1, tq, LANES), lambda bi, hp, i: (bi, i, qb + hp)),
            pl.BlockSpec((1, s, LANES), lambda bi, hp, i: (bi, 0, kb + hp)),
            pl.BlockSpec((1, s, LANES), lambda bi, hp, i: (bi, 0, vb + hp)),
        ],
        out_specs=pl.BlockSpec((1, tq, LANES), lambda bi, hp, i: (bi, i, hp)),
        out_shape=jax.ShapeDtypeStruct((b, s, SB_WIDTH), BF16),
        compiler_params=pltpu.CompilerParams(
            dimension_semantics=("parallel", "parallel", "arbitrary"), vmem_limit_bytes=VMEM_LIMIT),
        name="sb_attn",
    )(proj3, proj3, proj3)


DIL_TQ = 256
DIL_KP = DIL_WINDOW_KEYS


def _rel_bucket_idx(dist):
    max_exact = REL_BUCKETS // 2
    d = jnp.maximum(dist, 1).astype(F32)
    large = max_exact + (jnp.log(d / max_exact) / math.log(REL_MAX_DISTANCE / max_exact)
                         * (REL_BUCKETS - max_exact)).astype(I32)
    large = jnp.minimum(large, REL_BUCKETS - 1)
    return jnp.where(dist < max_exact, dist, large)


def _dil_bias_table(rel_bias, g, window, dilation):
    del window
    a = np.arange(DIL_TQ)[:, None]
    c = np.arange(DIL_KP + DIL_TQ)[None, :]
    step = a + DIL_KP - c
    valid = (step >= 0) & (step <= DIL_WINDOW_KEYS)
    bucket = _rel_bucket_idx(jnp.asarray(np.clip(step, 0, DIL_WINDOW_KEYS) * dilation, I32))
    heads = slice(g * DIL_HEADS_PER_GROUP, (g + 1) * DIL_HEADS_PER_GROUP)
    rb = rel_bias[:, heads].astype(F32)
    tab = jnp.zeros((DIL_HEADS_PER_GROUP,) + step.shape, F32)
    for bkt in range(REL_BUCKETS):
        tab = jnp.where(bucket[None] == bkt, rb[bkt][:, None, None], tab)
    return jnp.where(valid[None], tab, MASK_NEG)


def _dil_body(q_ref, kp_ref, kc_ref, vp_ref, vc_ref, bias_ref, o_ref, lse_ref, *, tq, scale):
    i = pl.program_id(2)
    q = q_ref[0]
    k2 = jnp.concatenate([kp_ref[0], kc_ref[0]], axis=0)
    v2 = jnp.concatenate([vp_ref[0], vc_ref[0]], axis=0)
    width = DIL_GROUP_WIDTH
    lane = lax.broadcasted_iota(I32, (tq, width), 1)
    keycol = lax.broadcasted_iota(I32, (tq, DIL_KP + tq), 1)
    no_prev = jnp.logical_and(i == 0, keycol < DIL_KP)
    zero = jnp.zeros_like(q)
    out = jnp.zeros((tq, width), F32)
    lse_out = jnp.zeros((tq, width), F32)
    for h in range(DIL_HEADS_PER_GROUP):
        in_head = jnp.logical_and(lane >= h * HEAD_DIM, lane < (h + 1) * HEAD_DIM)
        qm = jnp.where(in_head, q, zero)
        s = _dot_nt(qm, k2) * scale + bias_ref[h]
        s = jnp.where(no_prev, MASK_NEG, s)
        m = jnp.max(s, axis=-1, keepdims=True)
        p = jnp.exp(s - m)
        l = jnp.sum(p, axis=-1, keepdims=True)
        pv = _dot(p.astype(BF16), v2)
        out = jnp.where(in_head, pv / l, out)
        lse_out = jnp.where(in_head, m + jnp.log(l), lse_out)
    o_ref[0] = out
    lse_ref[0] = lse_out


def _dilated_group(qkv, bias_tab, g):
    b, dilation, sd, _ = qkv.shape
    tq = DIL_TQ
    width = DIL_GROUP_WIDTH
    blk = (1, None, tq, width)
    prev = (1, None, DIL_KP, width)
    per = tq // DIL_KP
    prev_row = lambda i: jnp.maximum(i * per - 1, 0)
    o, lse = pl.pallas_call(
        functools.partial(_dil_body, tq=tq, scale=1.0 / math.sqrt(HEAD_DIM)),
        grid=(b, dilation, sd // tq),
        in_specs=[
            pl.BlockSpec(blk, lambda bi, r, i: (bi, r, i, 0)),
            pl.BlockSpec(prev, lambda bi, r, i: (bi, r, prev_row(i), 1)),
            pl.BlockSpec(blk, lambda bi, r, i: (bi, r, i, 1)),
            pl.BlockSpec(prev, lambda bi, r, i: (bi, r, prev_row(i), 2)),
            pl.BlockSpec(blk, lambda bi, r, i: (bi, r, i, 2)),
            pl.BlockSpec((DIL_HEADS_PER_GROUP, tq, DIL_KP + tq), lambda bi, r, i: (0, 0, 0)),
        ],
        out_specs=[
            pl.BlockSpec((1, tq, width), lambda bi, r, i: (bi, i, r)),
            pl.BlockSpec((1, tq, width), lambda bi, r, i: (bi, i, r)),
        ],
        out_shape=[
            jax.ShapeDtypeStruct((b, sd, dilation * width), F32),
            jax.ShapeDtypeStruct((b, sd, dilation * width), F32),
        ],
        compiler_params=pltpu.CompilerParams(
            dimension_semantics=("parallel", "parallel", "arbitrary"), vmem_limit_bytes=VMEM_LIMIT),
        name=f"dil_attn_g{g}",
    )(qkv, qkv, qkv, qkv, qkv, bias_tab)
    t = b * dilation * sd
    return o.reshape(t, width), lse.reshape(t, width)


MERGE_TM = 256


def _merge_body(gp_ref, osb_ref, o0_ref, o1_ref, o2_ref, l0_ref, l1_ref, l2_ref, h_ref, bg_ref,
                wsb_ref, wdl_ref, wout_ref, g1_ref, b1_ref, wrh_ref, wrl_ref, h1_ref, h1p_ref, sc_ref):
    l0, l1, l2 = l0_ref[...], l1_ref[...], l2_ref[...]
    m = jnp.maximum(jnp.maximum(l0, l1), l2)
    e0, e1, e2 = jnp.exp(l0 - m), jnp.exp(l1 - m), jnp.exp(l2 - m)
    o_dl = (e0 * o0_ref[...] + e1 * o1_ref[...] + e2 * o2_ref[...]) / (e0 + e1 + e2)
    br_sb = _dot(osb_ref[...], wsb_ref[...])
    br_dl = _dot(o_dl.astype(BF16), wdl_ref[...])
    gates = jax.nn.sigmoid(gp_ref[...].astype(F32) + bg_ref[...])
    merged = gates[:, :D_MODEL] * br_sb + gates[:, D_MODEL:] * br_dl
    mix = _dot(merged.astype(BF16), wout_ref[...])
    h1 = _layer_norm(DN_ALPHA * h_ref[...] + mix, g1_ref[...], b1_ref[...])
    h1_ref[...] = h1
    h1p_ref[...] = _pack_rows(h1)
    h_hi = h1.astype(BF16)
    h_lo = (h1 - h_hi.astype(F32)).astype(BF16)
    logits = _dot_nt(wrh_ref[...], h_hi) + _dot_nt(wrh_ref[...], h_lo) + _dot_nt(wrl_ref[...], h_hi)
    sc_ref[...] = jax.nn.sigmoid(logits)


def _merge(gates, o_sb, dl, h, b_gate, w_br_sb, w_br_dil, w_out, g1, b1, wr_hi_t, wr_lo_t):
    t = h.shape[0]
    tm = MERGE_TM
    row = lambda w: pl.BlockSpec((tm, w), lambda i: (i, 0))
    full = lambda a: pl.BlockSpec(a.shape, lambda i: (0,) * a.ndim)
    args = (gates, o_sb, dl[0][0], dl[1][0], dl[2][0], dl[0][1], dl[1][1], dl[2][1], h, b_gate,
            w_br_sb, w_br_dil, w_out, g1, b1, wr_hi_t, wr_lo_t)
    in_specs = [row(GATE_COLS), row(SB_WIDTH)] + [row(DIL_GROUP_WIDTH)] * 6 + [row(D_MODEL)] \
        + [full(a) for a in args[9:]]
    return pl.pallas_call(
        _merge_body,
        grid=(t // tm,),
        in_specs=in_specs,
        out_specs=[row(D_MODEL), row(D_MODEL // 2), pl.BlockSpec((N_EXPERTS, tm), lambda i: (0, i))],
        out_shape=[jax.ShapeDtypeStruct((t, D_MODEL), F32), jax.ShapeDtypeStruct((t, D_MODEL // 2), U32),
                   jax.ShapeDtypeStruct((N_EXPERTS, t), F32)],
        compiler_params=pltpu.CompilerParams(
            dimension_semantics=("parallel",), vmem_limit_bytes=VMEM_LIMIT),
        name="merge_ln1",
    )(*args)


ROUTE_TR = 256


def _route_body(sc_ref, rb_ref, idx_ref, w_ref, rank_ref, cnt_ref, carry_ref, *, tr):
    @pl.when(pl.program_id(0) == 0)
    def _():
        carry_ref[...] = jnp.zeros_like(carry_ref)

    scores = sc_ref[...]
    biased = scores + rb_ref[...]
    erow = lax.broadcasted_iota(I32, (N_EXPERTS, tr), 0)
    neg = -jnp.inf

    def first_argmax(v, rows):
        m = jnp.max(v, axis=0, keepdims=True)
        return m, jnp.min(jnp.where(v == m, rows, N_EXPERTS), axis=0, keepdims=True)

    gslice = lambda a, g: a[g * GROUP_SIZE:(g + 1) * GROUP_SIZE, :]
    gscore = []
    grow = lax.broadcasted_iota(I32, (GROUP_SIZE, tr), 0)
    for g in range(N_GROUPS):
        vg, rg = gslice(biased, g), grow + g * GROUP_SIZE
        m1, i1 = first_argmax(vg, rg)
        m2 = jnp.max(jnp.where(rg == i1, neg, vg), axis=0, keepdims=True)
        gscore.append(m1 + m2)
    pieces = []
    for g in range(N_GROUPS):
        beaten = jnp.zeros((1, tr), I32)
        for g2 in range(N_GROUPS):
            if g2 == g:
                continue
            wins = gscore[g2] > gscore[g]
            if g2 < g:
                wins = jnp.logical_or(wins, gscore[g2] == gscore[g])
            beaten = beaten + wins.astype(I32)
        pieces.append(jnp.where(beaten < TOPK_GROUPS, gslice(biased, g), neg))
    masked = jnp.concatenate(pieces, axis=0)
    sel = jnp.zeros((N_EXPERTS, tr), jnp.bool_)
    picks = []
    for _ in range(TOP_K):
        _, ik = first_argmax(masked, erow)
        hit = erow == ik
        sel = jnp.logical_or(sel, hit)
        masked = jnp.where(hit, neg, masked)
        picks.append(ik)
    denom = jnp.sum(jnp.where(sel, scores, 0.0), axis=0, keepdims=True)

    r = lax.broadcasted_iota(I32, (tr, tr), 0)
    c = lax.broadcasted_iota(I32, (tr, tr), 1)
    earlier = (r < c).astype(BF16)
    sel_f = sel.astype(F32)
    before = _dot(sel_f.astype(BF16), earlier) + carry_ref[:, 0:1]
    new_carry = carry_ref[...] + jnp.sum(sel_f, axis=1, keepdims=True)
    carry_ref[...] = new_carry
    cnt_ref[...] = new_carry

    krow = lax.broadcasted_iota(I32, (TOP_K, tr), 0)
    idx_o = jnp.zeros((TOP_K, tr), I32)
    rank_o = jnp.zeros((TOP_K, tr), I32)
    w_o = jnp.zeros((TOP_K, tr), F32)
    for k, ik in enumerate(picks):
        hit = erow == ik
        wk = jnp.sum(jnp.where(hit, scores, 0.0), axis=0, keepdims=True) / denom * ROUTED_SCALE
        rk = jnp.sum(jnp.where(hit, before, 0.0), axis=0, keepdims=True).astype(I32)
        idx_o = jnp.where(krow == k, ik, idx_o)
        rank_o = jnp.where(krow == k, rk, rank_o)
        w_o = jnp.where(krow == k, wk, w_o)
    idx_ref[...] = idx_o
    rank_ref[...] = rank_o
    w_ref[...] = w_o


def _route(scores_t, router_bias_col):
    t = scores_t.shape[1]
    tr = ROUTE_TR
    col = lambda n: pl.BlockSpec((n, tr), lambda i: (0, i))
    return pl.pallas_call(
        functools.partial(_route_body, tr=tr),
        grid=(t // tr,),
        in_specs=[col(N_EXPERTS), pl.BlockSpec((N_EXPERTS, 1), lambda i: (0, 0))],
        out_specs=[col(TOP_K), col(TOP_K), col(TOP_K), pl.BlockSpec((N_EXPERTS, LANES), lambda i: (0, 0))],
        out_shape=[jax.ShapeDtypeStruct((TOP_K, t), I32), jax.ShapeDtypeStruct((TOP_K, t), F32),
                   jax.ShapeDtypeStruct((TOP_K, t), I32), jax.ShapeDtypeStruct((N_EXPERTS, LANES), F32)],
        scratch_shapes=[pltpu.VMEM((N_EXPERTS, LANES), F32)],
        compiler_params=pltpu.CompilerParams(
            dimension_semantics=("arbitrary",), vmem_limit_bytes=VMEM_LIMIT),
        name="route",
    )(scores_t, router_bias_col)


def _dest_body(idx_ref, rank_ref, ps_ref, dest_ref, *, tr):
    idx = idx_ref[...]
    erow = lax.broadcasted_iota(I32, (N_EXPERTS, tr), 0)
    starts = jnp.concatenate([ps_ref[...]] * (tr // LANES), axis=1)
    krow = lax.broadcasted_iota(I32, (TOP_K, tr), 0)
    out = rank_ref[...]
    for k in range(TOP_K):
        start_k = jnp.sum(jnp.where(erow == idx[k:k + 1, :], starts, 0), axis=0, keepdims=True)
        out = out + jnp.where(krow == k, start_k, 0)
    dest_ref[...] = out


def _dest(idx_t, rank_t, pstarts):
    t = idx_t.shape[1]
    tr = ROUTE_TR
    col = pl.BlockSpec((TOP_K, tr), lambda i: (0, i))
    starts = jnp.broadcast_to(pstarts[:, None], (N_EXPERTS, LANES))
    return pl.pallas_call(
        functools.partial(_dest_body, tr=tr),
        grid=(t // tr,),
        in_specs=[col, col, pl.BlockSpec((N_EXPERTS, LANES), lambda i: (0, 0))],
        out_specs=col,
        out_shape=jax.ShapeDtypeStruct((TOP_K, t), I32),
        compiler_params=pltpu.CompilerParams(
            dimension_semantics=("parallel",), vmem_limit_bytes=VMEM_LIMIT),
        name="dest",
    )(idx_t, rank_t, starts)


SC_WINDOW = 128
SC_COLS = 256
HALF = D_MODEL // 2


def _pack_rows(x):
    bits = lambda v: lax.bitcast_convert_type(v.astype(BF16).astype(F32), U32)
    return bits(x[:, HALF:]) | (bits(x[:, :HALF]) >> 16)


def _unpack_rows(w):
    lo = lax.bitcast_convert_type(w << 16, F32)
    hi = lax.bitcast_convert_type(w & jnp.uint32(0xFFFF0000), F32)
    return lo, hi


def _sc_mesh():
    return plsc.VectorSubcoreMesh(core_axis_name="core", subcore_axis_name="subcore")


def _sc_scatter_rows(x, dest_t, n_slots):
    t, width = x.shape

    @pl.kernel(out_type=jax.ShapeDtypeStruct((n_slots, width), x.dtype), mesh=_sc_mesh())
    def scatter(x_hbm, i_hbm, o_hbm):
        for k in range(TOP_K):
            for c in range(width // SC_COLS):
                def body(x_vmem, i_vmem, c=c):
                    pltpu.sync_copy(x_vmem, o_hbm.at[i_vmem.at[0], pl.ds(c * SC_COLS, SC_COLS)])

                pltpu.emit_pipeline(
                    body,
                    grid=(t // SC_WINDOW,),
                    in_specs=[pl.BlockSpec((SC_WINDOW, SC_COLS), index_map=lambda i, c=c: (i, c)),
                              pl.BlockSpec((1, SC_WINDOW), index_map=lambda i, k=k: (k, i))],
                    out_specs=[],
                    core_axis_name=("core", "subcore"),
                    dimension_semantics=(pltpu.PARALLEL,),
                )(x_hbm, i_hbm)

    return scatter(x, dest_t)


def _expert_body(ps_ref, cnt_ref, nu_ref, xs_hbm, wg_ref, wu_ref, wd_ref, ys_hbm,
                 xbuf, ybuf, insem, outsem, wgb_ref, wub_ref, wdb_ref):
    e = pl.program_id(0)
    bm = EXPERT_BM
    n_used = nu_ref[0]
    count = cnt_ref[e]
    n_blk = (count + bm - 1) // bm
    first = ps_ref[e] // bm

    def rows_of(g):
        return pl.ds(g * bm if isinstance(g, int) else pl.multiple_of(g * bm, bm), bm)

    def fetch(g):
        return pltpu.make_async_copy(xs_hbm.at[rows_of(g), :], xbuf.at[g % 2], insem.at[g % 2])

    def writeback(g):
        return pltpu.make_async_copy(ybuf.at[g % 2], ys_hbm.at[rows_of(g), :], outsem.at[g % 2])

    @pl.when(e == 0)
    def _():
        fetch(0).start()

    @pl.when(n_blk > 0)
    def _():
        wgb_ref[...] = wg_ref[0].astype(BF16)
        wub_ref[...] = wu_ref[0].astype(BF16)
        wdb_ref[...] = wd_ref[0].astype(BF16)

    def block(j, carry):
        g = first + j
        fetch(g).wait()
        pl.when(g + 1 < n_used)(fetch(g + 1).start)
        words = xbuf[g % 2]
        row = lax.broadcasted_iota(I32, words.shape, 0)
        lo, hi = _unpack_rows(jnp.where(row < count - j * bm, words, jnp.uint32(0)))
        lo, hi = lo.astype(BF16), hi.astype(BF16)
        gate = _dot(lo, wgb_ref[:HALF, :]) + _dot(hi, wgb_ref[HALF:, :])
        up = _dot(lo, wub_ref[:HALF, :]) + _dot(hi, wub_ref[HALF:, :])
        y = _pack_rows(_dot((_silu(gate) * up).astype(BF16), wdb_ref[...]))
        pl.when(g >= 2)(writeback(g - 2).wait)
        ybuf[g % 2] = y
        writeback(g).start()
        return carry

    lax.fori_loop(0, n_blk, block, 0)

    @pl.when(e == pl.num_programs(0) - 1)
    def _():
        pl.when(n_used >= 2)(writeback(n_used - 2).wait)
        writeback(n_used - 1).wait()


def _experts(pstarts, counts, n_used, xs, w_gate_e, w_up_e, w_down_e):
    n_slots = xs.shape[0]
    bm = EXPERT_BM
    wspec = lambda shape: pl.BlockSpec((1,) + shape, lambda e, ps, cnt, nu: (e, 0, 0))
    return pl.pallas_call(
        _expert_body,
        grid_spec=pltpu.PrefetchScalarGridSpec(
            num_scalar_prefetch=3,
            grid=(N_EXPERTS,),
            in_specs=[
                pl.BlockSpec(memory_space=pl.ANY),
                wspec((D_MODEL, EXPERT_HIDDEN)), wspec((D_MODEL, EXPERT_HIDDEN)), wspec((EXPERT_HIDDEN, D_MODEL)),
            ],
            out_specs=pl.BlockSpec(memory_space=pl.ANY),
            scratch_shapes=[pltpu.VMEM((2, bm, HALF), U32), pltpu.VMEM((2, bm, HALF), U32),
                            pltpu.SemaphoreType.DMA((2,)), pltpu.SemaphoreType.DMA((2,)),
                            pltpu.VMEM((D_MODEL, EXPERT_HIDDEN), BF16),
                            pltpu.VMEM((D_MODEL, EXPERT_HIDDEN), BF16),
                            pltpu.VMEM((EXPERT_HIDDEN, D_MODEL), BF16)],
        ),
        out_shape=jax.ShapeDtypeStruct((n_slots, HALF), U32),
        compiler_params=pltpu.CompilerParams(
            dimension_semantics=("arbitrary",), vmem_limit_bytes=VMEM_LIMIT),
        name="experts",
    )(pstarts, counts, n_used, xs, w_gate_e, w_up_e, w_down_e)


def _sc_gather_rows(table, idx_row):
    m = idx_row.shape[1]
    width = table.shape[1]

    @pl.kernel(out_type=jax.ShapeDtypeStruct((m, width), table.dtype), mesh=_sc_mesh())
    def gather(x_hbm, i_hbm, o_hbm):
        for c in range(width // SC_COLS):
            def body(i_vmem, o_vmem, c=c):
                pltpu.sync_copy(x_hbm.at[i_vmem.at[0], pl.ds(c * SC_COLS, SC_COLS)], o_vmem)

            pltpu.emit_pipeline(
                body,
                grid=(m // SC_WINDOW,),
                in_specs=[pl.BlockSpec((1, SC_WINDOW), index_map=lambda i: (0, i))],
                out_specs=[pl.BlockSpec((SC_WINDOW, SC_COLS), index_map=lambda i, c=c: (i, c))],
                core_axis_name=("core", "subcore"),
                dimension_semantics=(pltpu.PARALLEL,),
            )(i_hbm, o_hbm)

    return gather(table, idx_row)


COMBINE_TC = 256


def _shared_body(h1_ref, wgs_ref, wus_ref, wds_ref, o_ref):
    hb = h1_ref[...].astype(BF16)
    o_ref[...] = _dot((_silu(_dot(hb, wgs_ref[...])) * _dot(hb, wus_ref[...])).astype(BF16), wds_ref[...])


def _shared_expert(h1, wgs, wus, wds):
    t = h1.shape[0]
    tc = COMBINE_TC
    full = lambda a: pl.BlockSpec(a.shape, lambda i: (0,) * a.ndim)
    row = pl.BlockSpec((tc, D_MODEL), lambda i: (i, 0))
    return pl.pallas_call(
        _shared_body,
        grid=(t // tc,),
        in_specs=[row, full(wgs), full(wus), full(wds)],
        out_specs=row,
        out_shape=jax.ShapeDtypeStruct((t, D_MODEL), F32),
        compiler_params=pltpu.CompilerParams(
            dimension_semantics=("parallel",), vmem_limit_bytes=VMEM_LIMIT),
        name="shared_expert",
    )(h1, wgs, wus, wds)


def _combine_body(w_ref, h1_ref, sh_ref, *rest):
    rows = rest[:TOP_K]
    g2_ref, b2_ref, o_ref = rest[TOP_K:]
    h1 = h1_ref[...]
    shared = sh_ref[...]
    w = w_ref[...]
    acc_lo = jnp.zeros((h1.shape[0], HALF), F32)
    acc_hi = jnp.zeros((h1.shape[0], HALF), F32)
    for k in range(TOP_K):
        lo, hi = _unpack_rows(rows[k][...])
        acc_lo = acc_lo + w[:, k:k + 1] * lo
        acc_hi = acc_hi + w[:, k:k + 1] * hi
    routed = jnp.concatenate([acc_lo, acc_hi], axis=1)
    o_ref[...] = _layer_norm(DN_ALPHA * h1 + routed + shared, g2_ref[...], b2_ref[...])


def _combine(w_tok, h1, shared, gathered, g2, b2):
    t = h1.shape[0]
    tc = COMBINE_TC
    n_tiles = t // tc
    full = lambda a: pl.BlockSpec(a.shape, lambda i: (0,) * a.ndim)
    row_specs = [pl.BlockSpec((tc, HALF), lambda i, k=k: (k * n_tiles + i, 0)) for k in range(TOP_K)]
    return pl.pallas_call(
        _combine_body,
        grid=(n_tiles,),
        in_specs=[pl.BlockSpec((tc, TOP_K), lambda i: (i, 0)),
                  pl.BlockSpec((tc, D_MODEL), lambda i: (i, 0)),
                  pl.BlockSpec((tc, D_MODEL), lambda i: (i, 0))] + row_specs
        + [full(g2), full(b2)],
        out_specs=pl.BlockSpec((tc, D_MODEL), lambda i: (i, 0)),
        out_shape=jax.ShapeDtypeStruct((t, D_MODEL), F32),
        compiler_params=pltpu.CompilerParams(
            dimension_semantics=("parallel",), vmem_limit_bytes=VMEM_LIMIT),
        name="combine_ln2",
    )(w_tok, h1, shared, *([gathered] * TOP_K), g2, b2)


def _split_w_in(w):
    w = w.astype(BF16)
    sb = w[:, :SB_COLS]
    dq, dk, dv = (w[:, SB_COLS + j * DIL_WIDTH:SB_COLS + (j + 1) * DIL_WIDTH] for j in range(3))
    grp = lambda m, g: m[:, g * DIL_GROUP_WIDTH:(g + 1) * DIL_GROUP_WIDTH]
    dl = jnp.concatenate([grp(m, g) for g in range(N_DIL) for m in (dq, dk, dv)], axis=1)
    gates = w[:, SB_COLS + 3 * DIL_WIDTH:]
    return gates, sb, dl


def _moe(h1, h1_packed, scores_t, router_bias, w_gate_e, w_up_e, w_down_e, w_gate_s, w_up_s, w_down_s, g2, b2):
    t = h1.shape[0]
    idx_t, w_t, rank_t, counts = _route(scores_t, router_bias[:, None])
    counts = counts[:, 0].astype(I32)
    padded = (counts + EXPERT_BM - 1) // EXPERT_BM * EXPERT_BM
    pends = jnp.cumsum(padded)
    pstarts = (pends - padded).astype(I32)
    n_blocks = (t * TOP_K) // EXPERT_BM + N_EXPERTS
    n_used = (pends[-1:] // EXPERT_BM).astype(I32)
    dest_t = _dest(idx_t, rank_t, pstarts)
    xs = _sc_scatter_rows(h1_packed, dest_t, n_blocks * EXPERT_BM)
    shared = _shared_expert(h1, w_gate_s.astype(BF16), w_up_s.astype(BF16), w_down_s.astype(BF16))
    ys = _experts(pstarts, counts, n_used, xs, w_gate_e, w_up_e, w_down_e)
    gathered = _sc_gather_rows(ys, dest_t.reshape(1, TOP_K * t))
    return _combine(w_t.T, h1, shared, gathered, g2[None], b2[None])


def kernel(x, ln_in_g, ln_in_b, rel_bias, w_in, b_gate, w_br_sb, w_br_dil, w_out, ln1_g, ln1_b,
           w_router, router_bias, w_gate_e, w_up_e, w_down_e, w_gate_s, w_up_s, w_down_s,
           ln2_g, ln2_b):
    b, s, d = x.shape
    t = b * s
    n_layers = w_in.shape[0]
    h = x.reshape(t, d)
    ln_g, ln_b = ln_in_g[None], ln_in_b[None]
    bias_tabs = [_dil_bias_table(rel_bias, g, window, dil) for g, (window, dil) in enumerate(DIL_PATTERNS)]
    for l in range(n_layers):
        assert l == 0, "single-layer block"
        w_gate, w_sb, w_dl = _split_w_in(w_in[l])
        h, gates, sb, *dl_qkv = _inproj(h, ln_g, ln_b, w_gate, w_sb, w_dl, b)
        o_sb = _sb_attention(sb.reshape(b, s, SB_COLS)).reshape(t, SB_WIDTH)
        dl = [_dilated_group(dl_qkv[g], bias_tabs[g], g) for g in range(N_DIL)]
        wr = w_router[l].T
        wr_hi = wr.astype(BF16)
        wr_lo = (wr - wr_hi.astype(F32)).astype(BF16)
        h1, h1_packed, scores = _merge(gates, o_sb, dl, h, b_gate[l][None], w_br_sb[l].astype(BF16),
                                       w_br_dil[l].astype(BF16), w_out[l].astype(BF16),
                                       ln1_g[l][None], ln1_b[l][None], wr_hi, wr_lo)
        h = _moe(h1, h1_packed, scores, router_bias[l], w_gate_e[l], w_up_e[l], w_down_e[l],
                 w_gate_s[l], w_up_s[l], w_down_s[l], ln2_g[l], ln2_b[l])
    return h.reshape(b, s, d)
```
